```python
import jax, jax.numpy as jnp
from jax import lax
import numpy as np

D_MODEL = 2048
BATCH = 4
SEQ = 2048
DEPTH = 1
DEC_BATCH = 128
DEC_SEQ = 8
PAST_LEN = 16384
PAGE_SIZE = 128

D_FF = 5632
SSM_DINNER = D_MODEL
SSM_HEADDIM = 64
SSM_HEADS = SSM_DINNER // SSM_HEADDIM
SSM_GROUPS = 4
SSM_HPG = SSM_HEADS // SSM_GROUPS
SSM_STATE = 128
SSM_CONV = 4
SSM_CONV_DIM = SSM_DINNER + 2 * SSM_GROUPS * SSM_STATE
SSM_CHUNK = 128
CC_DIM = D_MODEL // 2
CC_KERNEL = 31
N_BRANCHES = 2
IN_COLS = SSM_DINNER + SSM_CONV_DIM + SSM_HEADS + 2 * CC_DIM + N_BRANCHES * D_MODEL
EPS = 1e-6

kernel_name = "hybrid_ssd_conformerconv_gated_macaron_step"


def rmsnorm(x, g):
    xf = x.astype(jnp.float32)
    y = xf * lax.rsqrt(jnp.mean(xf * xf, axis=-1, keepdims=True) + EPS)
    return (y * g.astype(jnp.float32)).astype(x.dtype)


def layernorm(x, g, b):
    xf = x.astype(jnp.float32)
    mu = jnp.mean(xf, axis=-1, keepdims=True)
    var = jnp.mean(jnp.square(xf - mu), axis=-1, keepdims=True)
    y = (xf - mu) * lax.rsqrt(var + EPS)
    return (y * g.astype(jnp.float32) + b.astype(jnp.float32)).astype(x.dtype)


def swiglu_ffn(x, wg, wu, wd):
    return (jax.nn.silu(x @ wg) * (x @ wu)) @ wd


def causal_depthwise_conv(x, buf, w, b):
    k = w.shape[0]
    xp = jnp.concatenate([buf.astype(x.dtype), x], axis=1)
    y = lax.conv_general_dilated(xp, w[:, None, :].astype(x.dtype), window_strides=(1,), padding='VALID',
                                 dimension_numbers=('NWC', 'WIO', 'NWC'), feature_group_count=x.shape[-1])
    new_buf = xp[:, xp.shape[1] - (k - 1):]
    return y + b.astype(x.dtype), new_buf


def ssd_scan(x, dt, A, B, C, h0):
    b, L = x.shape[0], x.shape[1]
    q = min(SSM_CHUNK, L)
    pad = (-L) % q
    if pad:
        padf = lambda t: jnp.pad(t, [(0, 0), (0, pad)] + [(0, 0)] * (t.ndim - 2))
        x, dt, B, C = padf(x), padf(dt), padf(B), padf(C)
    nc = (L + pad) // q
    G, E, P, N = SSM_GROUPS, SSM_HPG, SSM_HEADDIM, SSM_STATE
    xc = x.reshape(b, nc, q, G, E, P)
    dtc = dt.reshape(b, nc, q, G, E)
    Bc = B.reshape(b, nc, q, G, N)
    Cc = C.reshape(b, nc, q, G, N)
    cum = jnp.cumsum(dtc * A.reshape(G, E), axis=2)
    causal = jnp.tril(jnp.ones((q, q), dtype=bool))
    seg = cum[:, :, :, None] - cum[:, :, None, :]
    decay_ts = jnp.exp(jnp.where(causal[None, None, :, :, None, None], seg, -jnp.inf))
    cb = jnp.einsum('bctgn,bcsgn->bctsg', Cc, Bc)
    w_ts = cb[..., None] * decay_ts * dtc[:, :, None]
    y_diag = jnp.einsum('bctsge,bcsgep->bctgep', w_ts, xc)
    decay_end = jnp.exp(cum[:, :, -1:] - cum)
    chunk_states = jnp.einsum('bcsgn,bcsge,bcsgep->bcgepn', Bc, decay_end * dtc, xc)
    chunk_decay = jnp.exp(cum[:, :, -1])

    def step(h, inp):
        dec, st = inp
        return dec[..., None, None] * h + st, h

    h_final, h_starts = lax.scan(step, h0.reshape(b, G, E, P, N),
                                 (jnp.moveaxis(chunk_decay, 1, 0), jnp.moveaxis(chunk_states, 1, 0)))
    h_starts = jnp.moveaxis(h_starts, 0, 1)
    y_off = jnp.einsum('bctgn,bctge,bcgepn->bctgep', Cc, jnp.exp(cum), h_starts)
    y = (y_diag + y_off).reshape(b, nc * q, SSM_HEADS, P)[:, :L]
    return y, h_final.reshape(b, SSM_HEADS, P, N)


def token_mixer(h, st_ssm, st_sconv, st_cc, w_in, ssm_conv_w, ssm_conv_b, ssm_dt_bias, ssm_A_log, ssm_D,
                ssm_norm, w_ssd_out, cc_conv_w, cc_conv_b, cc_ln_g, cc_ln_b, w_cc_out, w_o):
    b, L, _ = h.shape
    proj = h @ w_in
    i0 = SSM_DINNER
    i1 = i0 + SSM_CONV_DIM
    i2 = i1 + SSM_HEADS
    i3 = i2 + 2 * CC_DIM
    z, xbc, dt_raw, glu_in, gate_logits = jnp.split(proj, [i0, i1, i2, i3], axis=-1)
    xbc, new_sconv = causal_depthwise_conv(xbc, st_sconv, ssm_conv_w, ssm_conv_b)
    xbc = jax.nn.silu(xbc)
    xs, Bm, Cm = jnp.split(xbc, [SSM_DINNER, SSM_DINNER + SSM_GROUPS * SSM_STATE], axis=-1)
    xs = xs.reshape(b, L, SSM_HEADS, SSM_HEADDIM).astype(jnp.float32)
    dt = jax.nn.softplus(dt_raw.astype(jnp.float32) + ssm_dt_bias.astype(jnp.float32))
    A = -jnp.exp(ssm_A_log.astype(jnp.float32))
    y, new_ssm = ssd_scan(xs, dt, A,
                          Bm.reshape(b, L, SSM_GROUPS, SSM_STATE).astype(jnp.float32),
                          Cm.reshape(b, L, SSM_GROUPS, SSM_STATE).astype(jnp.float32),
                          st_ssm.astype(jnp.float32))
    y = y + ssm_D.astype(jnp.float32)[:, None] * xs
    y = y.reshape(b, L, SSM_DINNER).astype(h.dtype)
    y = rmsnorm(y * jax.nn.silu(z), ssm_norm)
    ssd_out = y @ w_ssd_out
    ga, gb = jnp.split(glu_in, 2, axis=-1)
    u = ga * jax.nn.sigmoid(gb)
    u, new_cc = causal_depthwise_conv(u, st_cc, cc_conv_w, cc_conv_b)
    u = jax.nn.silu(layernorm(u, cc_ln_g, cc_ln_b))
    cc_out = u @ w_cc_out
    g_ssd, g_cc = jnp.split(jax.nn.sigmoid(gate_logits), 2, axis=-1)
    out = (g_ssd * ssd_out + g_cc * cc_out) @ w_o
    return out, new_ssm.astype(h.dtype), new_sconv, new_cc


def decoder_layer(x, st_ssm, st_sconv, st_cc, ffn_w, mix_w, norms):
    (f1g, f1u, f1d, f2g, f2u, f2d) = ffn_w
    (n_f1_pre, n_f1_post, n_mix_pre, n_mix_post, n_f2_pre, n_f2_post) = norms
    x = x + 0.5 * rmsnorm(swiglu_ffn(rmsnorm(x, n_f1_pre), f1g, f1u, f1d), n_f1_post)
    m, new_ssm, new_sconv, new_cc = token_mixer(rmsnorm(x, n_mix_pre), st_ssm, st_sconv, st_cc, *mix_w)
    x = x + rmsnorm(m, n_mix_post)
    x = x + 0.5 * rmsnorm(swiglu_ffn(rmsnorm(x, n_f2_pre), f2g, f2u, f2d), n_f2_post)
    return x, new_ssm, new_sconv, new_cc


def setup_inputs(seed: int = 0) -> dict:
    key = jax.random.key(seed)
    ks = iter(jax.random.split(key, 40))
    nrm = lambda shape, scale: jax.random.normal(next(ks), shape, jnp.float32) * scale
    gain = lambda shape: 1.0 + nrm(shape, 0.02)
    dtv = jnp.exp(jax.random.uniform(next(ks), (DEPTH, SSM_HEADS), jnp.float32, np.log(1e-3), np.log(1e-1)))
    inp = {
        "x_prompt": nrm((BATCH, SEQ, D_MODEL), 1.0),
        "x_sample": nrm((DEC_BATCH, DEC_SEQ, D_MODEL), 1.0),
        "state_ssm": nrm((DEPTH, DEC_BATCH, SSM_HEADS, SSM_HEADDIM, SSM_STATE), 0.5),
        "state_ssm_conv": nrm((DEPTH, DEC_BATCH, SSM_CONV - 1, SSM_CONV_DIM), 1.0),
        "state_cc_conv": nrm((DEPTH, DEC_BATCH, CC_KERNEL - 1, CC_DIM), 0.5),
        "ffn1_pre_norm": gain((DEPTH, D_MODEL)),
        "ffn1_post_norm": gain((DEPTH, D_MODEL)),
        "ffn1_w_gate": nrm((DEPTH, D_MODEL, D_FF), D_MODEL ** -0.5),
        "ffn1_w_up": nrm((DEPTH, D_MODEL, D_FF), D_MODEL ** -0.5),
        "ffn1_w_down": nrm((DEPTH, D_FF, D_MODEL), D_FF ** -0.5),
        "mix_pre_norm": gain((DEPTH, D_MODEL)),
        "mix_post_norm": gain((DEPTH, D_MODEL)),
        "w_in": nrm((DEPTH, D_MODEL, IN_COLS), D_MODEL ** -0.5),
        "ssm_conv_w": nrm((DEPTH, SSM_CONV, SSM_CONV_DIM), SSM_CONV ** -0.5),
        "ssm_conv_b": nrm((DEPTH, SSM_CONV_DIM), 0.02),
        "ssm_dt_bias": dtv + jnp.log(-jnp.expm1(-dtv)),
        "ssm_A_log": jnp.log(jax.random.uniform(next(ks), (DEPTH, SSM_HEADS), jnp.float32, 1.0, 16.0)),
        "ssm_D": gain((DEPTH, SSM_HEADS)),
        "ssm_norm": gain((DEPTH, SSM_DINNER)),
        "w_ssd_out": nrm((DEPTH, SSM_DINNER, D_MODEL), SSM_DINNER ** -0.5),
        "cc_conv_w": nrm((DEPTH, CC_KERNEL, CC_DIM), CC_KERNEL ** -0.5),
        "cc_conv_b": nrm((DEPTH, CC_DIM), 0.02),
        "cc_ln_g": gain((DEPTH, CC_DIM)),
        "cc_ln_b": nrm((DEPTH, CC_DIM), 0.02),
        "w_cc_out": nrm((DEPTH, CC_DIM, D_MODEL), CC_DIM ** -0.5),
        "w_o": nrm((DEPTH, D_MODEL, D_MODEL), D_MODEL ** -0.5),
        "ffn2_pre_norm": gain((DEPTH, D_MODEL)),
        "ffn2_post_norm": gain((DEPTH, D_MODEL)),
        "ffn2_w_gate": nrm((DEPTH, D_MODEL, D_FF), D_MODEL ** -0.5),
        "ffn2_w_up": nrm((DEPTH, D_MODEL, D_FF), D_MODEL ** -0.5),
        "ffn2_w_down": nrm((DEPTH, D_FF, D_MODEL), D_FF ** -0.5),
    }
    return inp


def reference(x_prompt, x_sample, state_ssm, state_ssm_conv, state_cc_conv,
              ffn1_pre_norm, ffn1_post_norm, ffn1_w_gate, ffn1_w_up, ffn1_w_down,
              mix_pre_norm, mix_post_norm, w_in, ssm_conv_w, ssm_conv_b, ssm_dt_bias, ssm_A_log, ssm_D,
              ssm_norm, w_ssd_out, cc_conv_w, cc_conv_b, cc_ln_g, cc_ln_b, w_cc_out, w_o,
              ffn2_pre_norm, ffn2_post_norm, ffn2_w_gate, ffn2_w_up, ffn2_w_down):
    dt_p = x_prompt.dtype
    xp, xs = x_prompt, x_sample
    p_ssm, p_sconv, p_cc = [], [], []
    s_ssm, s_sconv, s_cc = [], [], []
    for l in range(DEPTH):
        ffn_w = (ffn1_w_gate[l], ffn1_w_up[l], ffn1_w_down[l], ffn2_w_gate[l], ffn2_w_up[l], ffn2_w_down[l])
        norms = (ffn1_pre_norm[l], ffn1_post_norm[l], mix_pre_norm[l], mix_post_norm[l],
                 ffn2_pre_norm[l], ffn2_post_norm[l])
        mix_w = (w_in[l], ssm_conv_w[l], ssm_conv_b[l], ssm_dt_bias[l], ssm_A_log[l], ssm_D[l], ssm_norm[l],
                 w_ssd_out[l], cc_conv_w[l], cc_conv_b[l], cc_ln_g[l], cc_ln_b[l], w_cc_out[l], w_o[l])
        z_ssm = jnp.zeros((BATCH, SSM_HEADS, SSM_HEADDIM, SSM_STATE), dt_p)
        z_sconv = jnp.zeros((BATCH, SSM_CONV - 1, SSM_CONV_DIM), dt_p)
        z_cc = jnp.zeros((BATCH, CC_KERNEL - 1, CC_DIM), dt_p)
        xp, a, b_, c = decoder_layer(xp, z_ssm, z_sconv, z_cc, ffn_w, mix_w, norms)
        p_ssm.append(a); p_sconv.append(b_); p_cc.append(c)
        xs, a, b_, c = decoder_layer(xs, state_ssm[l], state_ssm_conv[l], state_cc_conv[l], ffn_w, mix_w, norms)
        s_ssm.append(a); s_sconv.append(b_); s_cc.append(c)
    return (xp, xs, jnp.stack(p_ssm), jnp.stack(p_sconv), jnp.stack(p_cc),
            jnp.stack(s_ssm), jnp.stack(s_sconv), jnp.stack(s_cc))
```

```python
import functools

import jax
import jax.numpy as jnp
from jax import lax
from jax.experimental import pallas as pl
from jax.experimental.pallas import tpu as pltpu

F32 = jnp.float32
BF16 = jnp.bfloat16

D_MODEL = 2048
D_FF = 5632
BATCH, SEQ = 4, 2048
DEC_BATCH, DEC_SEQ = 128, 8
N_PROMPT = BATCH * SEQ
N_SAMPLE = DEC_BATCH * DEC_SEQ
N_TOK = N_PROMPT + N_SAMPLE
HEADS, HEADDIM, GROUPS, STATE = 32, 64, 4, 128
HPG = HEADS // GROUPS
DINNER = HEADS * HEADDIM
GW = HPG * HEADDIM
SSM_K = 4
CONV_DIM = DINNER + 2 * GROUPS * STATE
CC_DIM = D_MODEL // 2
CC_K = 31
EPS = 1e-6

LANES = 128
CHUNK = 128
SEGS = CHUNK // DEC_SEQ
VMEM_LIMIT = 48 * 1024 * 1024


def _cparams(sem):
    return pltpu.CompilerParams(dimension_semantics=sem, vmem_limit_bytes=VMEM_LIMIT)


def _rms(x, g):
    ms = jnp.mean(x * x, axis=-1, keepdims=True)
    return (x * lax.rsqrt(ms + EPS)) * g


def _dot(a, b):
    return jnp.dot(a, b, preferred_element_type=F32)


def _split3(x):
    hi = x.astype(BF16)
    r = x - hi.astype(F32)
    mid = r.astype(BF16)
    lo = (r - mid.astype(F32)).astype(BF16)
    return hi, mid, lo


def _dot01_rhs(x, m01):
    hi, mid, lo = _split3(x)
    return _dot(hi, m01) + _dot(mid, m01) + _dot(lo, m01)


def _dot01_lhs(m01, x):
    hi, mid, lo = _split3(x)
    return _dot(m01, hi) + _dot(m01, mid) + _dot(m01, lo)


def _ffn_kernel(x_ref, pre_ref, post_ref, nxt_ref, wg_ref, wu_ref, wd_ref, *rest, emit_h):
    if emit_h:
        xo_ref, ho_ref, h_scr, acc_scr = rest
    else:
        xo_ref, h_scr, acc_scr = rest
    j = pl.program_id(1)
    nj = pl.num_programs(1)

    @pl.when(j == 0)
    def _():
        h_scr[...] = _rms(x_ref[...], pre_ref[...]).astype(BF16)

    h = h_scr[...]
    g = _dot(h, wg_ref[...])
    u = _dot(h, wu_ref[...])
    a = (jax.nn.silu(g) * u).astype(BF16)
    part = _dot(a, wd_ref[...])

    @pl.when(j == 0)
    def _():
        acc_scr[...] = part

    @pl.when(j > 0)
    def _():
        acc_scr[...] += part

    @pl.when(j == nj - 1)
    def _():
        xn = x_ref[...] + 0.5 * _rms(acc_scr[...], post_ref[...])
        xo_ref[...] = xn
        if emit_h:
            ho_ref[...] = _rms(xn, nxt_ref[...]).astype(BF16)


def _ffn(x, pre, post, nxt, wg, wu, wd, *, emit_h, tm=512, tf=512):
    m = x.shape[0]
    grid = (m // tm, D_FF // tf)
    row = lambda i, j: (i, 0)
    vec = pl.BlockSpec((1, D_MODEL), lambda i, j: (0, 0))
    out_shape = [jax.ShapeDtypeStruct((m, D_MODEL), F32)]
    out_specs = [pl.BlockSpec((tm, D_MODEL), row)]
    if emit_h:
        out_shape.append(jax.ShapeDtypeStruct((m, D_MODEL), BF16))
        out_specs.append(pl.BlockSpec((tm, D_MODEL), row))
    return pl.pallas_call(
        functools.partial(_ffn_kernel, emit_h=emit_h),
        grid=grid,
        in_specs=[
            pl.BlockSpec((tm, D_MODEL), row), vec, vec, vec,
            pl.BlockSpec((D_MODEL, tf), lambda i, j: (0, j)),
            pl.BlockSpec((D_MODEL, tf), lambda i, j: (0, j)),
            pl.BlockSpec((tf, D_MODEL), lambda i, j: (j, 0)),
        ],
        out_specs=out_specs,
        out_shape=out_shape,
        scratch_shapes=[pltpu.VMEM((tm, D_MODEL), BF16), pltpu.VMEM((tm, D_MODEL), F32)],
        compiler_params=_cparams(("parallel", "arbitrary")),
        name="ffn_e" if emit_h else "ffn",
    )(x, pre, post, nxt, wg, wu, wd)


def _mm_plain_kernel(h_ref, w_ref, o_ref):
    o_ref[...] = _dot(h_ref[...], w_ref[...]).astype(o_ref.dtype)


def _mm_dt_kernel(h_ref, w_ref, b_ref, o_ref):
    o_ref[...] = jax.nn.softplus(_dot(h_ref[...], w_ref[...]) + b_ref[...])


def _mm_glu_kernel(h_ref, wa_ref, wb_ref, o_ref):
    h = h_ref[...]
    o_ref[...] = _dot(h, wa_ref[...]) * jax.nn.sigmoid(_dot(h, wb_ref[...]))


def _mm(kernel, h, ws, extras, n_out, out_dtype, name, *, tm=1024, tn=512):
    m, k = h.shape
    tn = min(tn, n_out)
    grid = (m // tm, n_out // tn)
    in_specs = [pl.BlockSpec((tm, k), lambda i, j: (i, 0))]
    in_specs += [pl.BlockSpec((k, tn), lambda i, j: (0, j)) for _ in ws]
    in_specs += [pl.BlockSpec((1, tn), lambda i, j: (0, j)) for _ in extras]
    return pl.pallas_call(
        kernel,
        grid=grid,
        in_specs=in_specs,
        out_specs=pl.BlockSpec((tm, tn), lambda i, j: (i, j)),
        out_shape=jax.ShapeDtypeStruct((m, n_out), out_dtype),
        compiler_params=_cparams(("parallel", "arbitrary")),
        name=name,
    )(h, *ws, *extras)


def _block_masks(seg_len):
    t = lax.broadcasted_iota(jnp.int32, (CHUNK, CHUNK), 0)
    s = lax.broadcasted_iota(jnp.int32, (CHUNK, CHUNK), 1)
    if seg_len == CHUNK:
        same = t >= 0
    else:
        sh = seg_len.bit_length() - 1
        same = (t >> sh) == (s >> sh)
    causal = jnp.logical_and(same, s <= t)
    return same, causal


def _conv4_silu(window, cw_ref, cb_ref):
    acc = cb_ref[...] + cw_ref[0:1, :] * window(0)
    for k in range(1, SSM_K):
        acc = acc + cw_ref[k:k + 1, :] * window(k)
    return jax.nn.silu(acc)


def _ssd_intra(xc, dt, alog_row, e01, seg_len):
    same, causal = _block_masks(seg_len)
    xs = xc[:, :DINNER]
    bm = xc[:, DINNER:DINNER + GROUPS * STATE]
    cm = xc[:, DINNER + GROUPS * STATE:]
    a_row = -jnp.exp(alog_row)
    dta = dt * a_row
    causal01 = jnp.where(causal, 1.0, 0.0).astype(BF16)
    same01 = jnp.where(same, 1.0, 0.0).astype(BF16)
    cum = _dot01_lhs(causal01, dta)
    cum_end = _dot01_lhs(same01, dta)
    cum_t = cum.T
    dt_t = dt.T
    ecum_x = _dot01_rhs(jnp.exp(cum), e01)
    coef_x = _dot01_rhs(jnp.exp(cum_end - cum) * dt, e01)

    lane = lax.broadcasted_iota(jnp.int32, (CHUNK, LANES), 1)
    first_head = lane < HEADDIM
    ys = []
    for g in range(GROUPS):
        cg = cm[:, g * STATE:(g + 1) * STATE].astype(BF16)
        bg = bm[:, g * STATE:(g + 1) * STATE].astype(BF16)
        cb = lax.dot_general(cg, bg, (((1,), (1,)), ((), ())), preferred_element_type=F32)
        for p in range(HPG // 2):
            ws = []
            for e in (g * HPG + 2 * p, g * HPG + 2 * p + 1):
                seg = cum[:, e:e + 1] - cum_t[e:e + 1, :]
                dec = jnp.exp(jnp.where(causal, seg, -jnp.inf))
                ws.append(((cb * dec) * dt_t[e:e + 1, :]).astype(BF16))
            w2 = jnp.concatenate(ws, axis=1)
            col = (g * HPG + 2 * p) * HEADDIM
            x2 = xs[:, col:col + LANES]
            r2 = jnp.concatenate([jnp.where(first_head, x2, 0.0),
                                  jnp.where(first_head, 0.0, x2)], axis=0).astype(BF16)
            ys.append(_dot(w2, r2))
    y_diag = jnp.concatenate(ys, axis=1)
    return xs, cm, bm, y_diag, coef_x, ecum_x


def _gated_norm(y, z, nrm_row):
    return _rms(y * jax.nn.silu(z), nrm_row).astype(BF16)


def _ssd_prompt_kernel(xbc_ref, dt_ref, z_ref, e_ref, cw_ref, cb_ref, alog_ref, dx_ref, nrm_ref,
                       yn_ref, hout_ref, xp_ref, ht_ref):
    c = pl.program_id(1)

    @pl.when(c == 0)
    def _():
        xp_ref[0:8, :] = jnp.zeros((8, CONV_DIM), F32)
        ht_ref[...] = jnp.zeros_like(ht_ref)

    xp_ref[8:8 + CHUNK, :] = xbc_ref[...]
    xc = _conv4_silu(lambda k: xp_ref[5 + k:5 + k + CHUNK, :], cw_ref, cb_ref)
    xp_ref[0:8, :] = xbc_ref[CHUNK - 8:CHUNK, :]

    xs, cm, bm, y_diag, coef_x, ecum_x = _ssd_intra(xc, dt_ref[...], alog_ref[...], e_ref[...], CHUNK)
    xcoef = (xs * coef_x).astype(BF16)
    dec_row = ecum_x[CHUNK - 1:CHUNK, :]
    y_off = []
    for g in range(GROUPS):
        gs = slice(g * GW, (g + 1) * GW)
        ht_g = ht_ref[:, gs]
        cg = cm[:, g * STATE:(g + 1) * STATE].astype(BF16)
        y_off.append(_dot(cg, ht_g.astype(BF16)))
        bg_t = bm[:, g * STATE:(g + 1) * STATE].T.astype(BF16)
        ht_ref[:, gs] = ht_g * dec_row[:, gs] + _dot(bg_t, xcoef[:, gs])
    y = y_diag + jnp.concatenate(y_off, axis=1) * ecum_x + dx_ref[...] * xs
    yn_ref[...] = _gated_norm(y, z_ref[...], nrm_ref[...])

    @pl.when(c == pl.num_programs(1) - 1)
    def _():
        hout_ref[...] = ht_ref[...].T


def _ssd_prompt(xbc, dt, z, e01, cw, cb, alog, dx, nrm):
    nc = SEQ // CHUNK
    blk = lambda b, c: (b * nc + c, 0)
    const = lambda b, c: (0, 0)
    return pl.pallas_call(
        _ssd_prompt_kernel,
        grid=(BATCH, nc),
        in_specs=[
            pl.BlockSpec((CHUNK, CONV_DIM), blk),
            pl.BlockSpec((CHUNK, LANES), blk),
            pl.BlockSpec((CHUNK, DINNER), blk),
            pl.BlockSpec((LANES, DINNER), const),
            pl.BlockSpec((SSM_K, CONV_DIM), const),
            pl.BlockSpec((1, CONV_DIM), const),
            pl.BlockSpec((1, LANES), const),
            pl.BlockSpec((1, DINNER), const),
            pl.BlockSpec((1, DINNER), const),
        ],
        out_specs=[
            pl.BlockSpec((CHUNK, DINNER), blk),
            pl.BlockSpec((None, DINNER, STATE), lambda b, c: (b, 0, 0)),
        ],
        out_shape=[
            jax.ShapeDtypeStruct((N_PROMPT, DINNER), BF16),
            jax.ShapeDtypeStruct((BATCH, DINNER, STATE), F32),
        ],
        scratch_shapes=[pltpu.VMEM((CHUNK + 8, CONV_DIM), F32), pltpu.VMEM((STATE, DINNER), F32)],
        compiler_params=_cparams(("parallel", "arbitrary")),
        name="ssd_prompt",
    )(xbc, dt, z, e01, cw, cb, alog, dx, nrm)


def _ssd_sample_kernel(xbc_ref, cst_ref, dt_ref, z_ref, h0_ref, e_ref, cw_ref, cb_ref, alog_ref,
                       dx_ref, nrm_ref, yn_ref, hout_ref,
                       xp_ref, cm_ref, bt_ref, xcoef_ref, ecx_ref, y_ref):
    j = pl.program_id(1)

    @pl.when(j == 0)
    def _():
        xp_ref[:, 8 - (SSM_K - 1):8, :] = cst_ref[...]
        xp_ref[:, 8:8 + DEC_SEQ, :] = xbc_ref[...]
        xc = _conv4_silu(
            lambda k: xp_ref[:, 5 + k:5 + k + DEC_SEQ, :].reshape(CHUNK, CONV_DIM), cw_ref, cb_ref)
        xs, cm, bm, y_diag, coef_x, ecum_x = _ssd_intra(
            xc, dt_ref[...], alog_ref[...], e_ref[...], DEC_SEQ)
        cm_ref[...] = cm
        for g in range(GROUPS):
            bt_ref[g] = bm[:, g * STATE:(g + 1) * STATE].T
        xcoef_ref[...] = (xs * coef_x).astype(BF16)
        ecx_ref[...] = ecum_x
        y_ref[...] = y_diag + dx_ref[...] * xs

    r0 = pl.multiple_of(j * DEC_SEQ, DEC_SEQ)
    rows = pl.ds(r0, DEC_SEQ)
    ht = h0_ref[...].T
    ht_b = ht.astype(BF16)
    ecx = ecx_ref[rows, :]
    cj = cm_ref[rows, :].astype(BF16)
    col = lax.broadcasted_iota(jnp.int32, (STATE, CHUNK), 1)
    mine = jnp.logical_and(col >= r0, col < r0 + DEC_SEQ)
    y_off, new = [], []
    for g in range(GROUPS):
        gs = slice(g * GW, (g + 1) * GW)
        y_off.append(_dot(cj[:, g * STATE:(g + 1) * STATE], ht_b[:, gs]))
        bg_t = jnp.where(mine, bt_ref[g], 0.0).astype(BF16)
        new.append(_dot(bg_t, xcoef_ref[:, gs]))
    y_ref[rows, :] += jnp.concatenate(y_off, axis=1) * ecx
    ht_new = ht * ecx[DEC_SEQ - 1:DEC_SEQ, :] + jnp.concatenate(new, axis=1)
    hout_ref[...] = ht_new.T

    @pl.when(j == pl.num_programs(1) - 1)
    def _():
        yn_ref[...] = _gated_norm(y_ref[...], z_ref[...], nrm_ref[...])


def _ssd_sample(xbc3, cst, dt, z, h0, e01, cw, cb, alog, dx, nrm):
    nb = DEC_BATCH // SEGS
    off = N_PROMPT // CHUNK
    const = lambda i, j: (0, 0)
    tok = lambda i, j: (off + i, 0)
    seq = lambda i, j: (i * SEGS + j, 0, 0)
    return pl.pallas_call(
        _ssd_sample_kernel,
        grid=(nb, SEGS),
        in_specs=[
            pl.BlockSpec((SEGS, DEC_SEQ, CONV_DIM), lambda i, j: (i, 0, 0)),
            pl.BlockSpec((SEGS, SSM_K - 1, CONV_DIM), lambda i, j: (i, 0, 0)),
            pl.BlockSpec((CHUNK, LANES), tok),
            pl.BlockSpec((CHUNK, DINNER), tok),
            pl.BlockSpec((None, DINNER, STATE), seq),
            pl.BlockSpec((LANES, DINNER), const),
            pl.BlockSpec((SSM_K, CONV_DIM), const),
            pl.BlockSpec((1, CONV_DIM), const),
            pl.BlockSpec((1, LANES), const),
            pl.BlockSpec((1, DINNER), const),
            pl.BlockSpec((1, DINNER), const),
        ],
        out_specs=[
            pl.BlockSpec((CHUNK, DINNER), lambda i, j: (i, 0)),
            pl.BlockSpec((None, DINNER, STATE), seq),
        ],
        out_shape=[
            jax.ShapeDtypeStruct((N_SAMPLE, DINNER), BF16),
            jax.ShapeDtypeStruct((DEC_BATCH, DINNER, STATE), F32),
        ],
        scratch_shapes=[
            pltpu.VMEM((SEGS, 8 + DEC_SEQ, CONV_DIM), F32),
            pltpu.VMEM((CHUNK, GROUPS * STATE), F32),
            pltpu.VMEM((GROUPS, STATE, CHUNK), F32),
            pltpu.VMEM((CHUNK, DINNER), BF16),
            pltpu.VMEM((CHUNK, DINNER), F32),
            pltpu.VMEM((CHUNK, DINNER), F32),
        ],
        compiler_params=_cparams(("parallel", "arbitrary")),
        name="ssd_sample",
    )(xbc3, cst, dt, z, h0, e01, cw, cb, alog, dx, nrm)


CC_PAD = 32


def _cc_kernel(*refs, n_seg, seg_len, rb, carry):
    if carry:
        u_ref, w_ref, b_ref, g_ref, be_ref, o_ref, xp_ref, res_ref = refs
    else:
        u_ref, st_ref, w_ref, b_ref, g_ref, be_ref, o_ref, xp_ref, res_ref = refs
    first = CC_PAD - (CC_K - 1)
    if carry:
        @pl.when(pl.program_id(1) == 0)
        def _():
            xp_ref[:, 0:CC_PAD, :] = jnp.zeros((n_seg, CC_PAD, CC_DIM), F32)
    else:
        xp_ref[:, first:CC_PAD, :] = st_ref[...]
    xp_ref[:, CC_PAD:CC_PAD + seg_len, :] = u_ref[...]

    for s in range(n_seg):
        for r in range(seg_len // rb):
            base = first + r * rb
            acc = b_ref[...] + w_ref[0:1, :] * xp_ref[s, base:base + rb, :]
            for k in range(1, CC_K):
                acc = acc + w_ref[k:k + 1, :] * xp_ref[s, base + k:base + k + rb, :]
            mu = jnp.mean(acc, axis=-1, keepdims=True)
            d = acc - mu
            var = jnp.mean(d * d, axis=-1, keepdims=True)
            yv = (d * lax.rsqrt(var + EPS)) * g_ref[...] + be_ref[...]
            row = s * seg_len + r * rb
            res_ref[row:row + rb, :] = jax.nn.silu(yv)
    o_ref[...] = res_ref[...].astype(BF16)
    if carry:
        xp_ref[:, 0:CC_PAD, :] = xp_ref[:, seg_len:seg_len + CC_PAD, :]


def _cc_prompt(u3, w, b, g, be, *, tq):
    nc = SEQ // tq
    const = lambda i, c: (0, 0)
    return pl.pallas_call(
        functools.partial(_cc_kernel, n_seg=1, seg_len=tq, rb=16, carry=True),
        grid=(BATCH, nc),
        in_specs=[
            pl.BlockSpec((1, tq, CC_DIM), lambda i, c: (i * nc + c, 0, 0)),
            pl.BlockSpec((CC_K, CC_DIM), const),
            pl.BlockSpec((1, CC_DIM), const),
            pl.BlockSpec((1, CC_DIM), const),
            pl.BlockSpec((1, CC_DIM), const),
        ],
        out_specs=pl.BlockSpec((tq, CC_DIM), lambda i, c: (i * nc + c, 0)),
        out_shape=jax.ShapeDtypeStruct((N_PROMPT, CC_DIM), BF16),
        scratch_shapes=[pltpu.VMEM((1, CC_PAD + tq, CC_DIM), F32), pltpu.VMEM((tq, CC_DIM), F32)],
        compiler_params=_cparams(("parallel", "arbitrary")),
        name="cc_prompt",
    )(u3, w, b, g, be)


def _cc_sample(u3, st, w, b, g, be, *, n_seg):
    const = lambda i: (0, 0)
    rows = n_seg * DEC_SEQ
    return pl.pallas_call(
        functools.partial(_cc_kernel, n_seg=n_seg, seg_len=DEC_SEQ, rb=DEC_SEQ, carry=False),
        grid=(DEC_BATCH // n_seg,),
        in_specs=[
            pl.BlockSpec((n_seg, DEC_SEQ, CC_DIM), lambda i: (i, 0, 0)),
            pl.BlockSpec((n_seg, CC_K - 1, CC_DIM), lambda i: (i, 0, 0)),
            pl.BlockSpec((CC_K, CC_DIM), const),
            pl.BlockSpec((1, CC_DIM), const),
            pl.BlockSpec((1, CC_DIM), const),
            pl.BlockSpec((1, CC_DIM), const),
        ],
        out_specs=pl.BlockSpec((rows, CC_DIM), lambda i: (i, 0)),
        out_shape=jax.ShapeDtypeStruct((N_SAMPLE, CC_DIM), BF16),
        scratch_shapes=[pltpu.VMEM((n_seg, CC_PAD + DEC_SEQ, CC_DIM), F32),
                        pltpu.VMEM((rows, CC_DIM), F32)],
        compiler_params=_cparams(("parallel",)),
        name="cc_sample",
    )(u3, st, w, b, g, be)


def _merge_kernel(h_ref, yn_ref, uc_ref, wgs_ref, wgc_ref, wso_ref, wco_ref, o_ref):
    h = h_ref[...]
    g_ssd = jax.nn.sigmoid(_dot(h, wgs_ref[...]))
    g_cc = jax.nn.sigmoid(_dot(h, wgc_ref[...]))
    ssd_out = _dot(yn_ref[...], wso_ref[...])
    cc_out = _dot(uc_ref[...], wco_ref[...])
    o_ref[...] = (g_ssd * ssd_out + g_cc * cc_out).astype(BF16)


def _merge(h, yn, uc, wgs, wgc, wso, wco, *, tm=1024, tn=512):
    m = h.shape[0]
    row = lambda i, j: (i, 0)
    colw = lambda i, j: (0, j)
    return pl.pallas_call(
        _merge_kernel,
        grid=(m // tm, D_MODEL // tn),
        in_specs=[
            pl.BlockSpec((tm, D_MODEL), row),
            pl.BlockSpec((tm, DINNER), row),
            pl.BlockSpec((tm, CC_DIM), row),
            pl.BlockSpec((D_MODEL, tn), colw),
            pl.BlockSpec((D_MODEL, tn), colw),
            pl.BlockSpec((DINNER, tn), colw),
            pl.BlockSpec((CC_DIM, tn), colw),
        ],
        out_specs=pl.BlockSpec((tm, tn), lambda i, j: (i, j)),
        out_shape=jax.ShapeDtypeStruct((m, D_MODEL), BF16),
        compiler_params=_cparams(("parallel", "arbitrary")),
        name="merge",
    )(h, yn, uc, wgs, wgc, wso, wco)


def _oproj_kernel(m_ref, x_ref, w_ref, nrm_ref, o_ref):
    o_ref[...] = x_ref[...] + _rms(_dot(m_ref[...], w_ref[...]), nrm_ref[...])


def _oproj(mg, x, w, nrm, *, tm=512):
    m = x.shape[0]
    row = lambda i: (i, 0)
    const = lambda i: (0, 0)
    return pl.pallas_call(
        _oproj_kernel,
        grid=(m // tm,),
        in_specs=[
            pl.BlockSpec((tm, D_MODEL), row),
            pl.BlockSpec((tm, D_MODEL), row),
            pl.BlockSpec((D_MODEL, D_MODEL), const),
            pl.BlockSpec((1, D_MODEL), const),
        ],
        out_specs=pl.BlockSpec((tm, D_MODEL), row),
        out_shape=jax.ShapeDtypeStruct((m, D_MODEL), F32),
        compiler_params=_cparams(("parallel",)),
        name="oproj",
    )(mg, x, w, nrm)


def _row(v):
    return v.reshape(1, -1).astype(F32)


def _pad_lanes(v):
    return jnp.pad(v.astype(F32), (0, LANES - v.shape[0])).reshape(1, LANES)


def kernel(x_prompt, x_sample, state_ssm, state_ssm_conv, state_cc_conv, ffn1_pre_norm, ffn1_post_norm, ffn1_w_gate, ffn1_w_up, ffn1_w_down, mix_pre_norm, mix_post_norm, w_in, ssm_conv_w, ssm_conv_b, ssm_dt_bias, ssm_A_log, ssm_D, ssm_norm, w_ssd_out, cc_conv_w, cc_conv_b, cc_ln_g, cc_ln_b, w_cc_out, w_o, ffn2_pre_norm, ffn2_post_norm, ffn2_w_gate, ffn2_w_up, ffn2_w_down):
    l = 0
    bf = lambda w: w.astype(BF16)
    x = jnp.concatenate([x_prompt.reshape(N_PROMPT, D_MODEL), x_sample.reshape(N_SAMPLE, D_MODEL)], axis=0)

    w = w_in[l]
    c0 = DINNER
    c1 = c0 + CONV_DIM
    c2 = c1 + HEADS
    c3 = c2 + CC_DIM
    c4 = c3 + CC_DIM
    c5 = c4 + D_MODEL
    w_z, w_xbc = bf(w[:, :c0]), bf(w[:, c0:c1])
    w_dt = bf(jnp.pad(w[:, c1:c2], ((0, 0), (0, LANES - HEADS))))
    w_ga, w_gb = bf(w[:, c2:c3]), bf(w[:, c3:c4])
    w_gs, w_gc = bf(w[:, c4:c5]), bf(w[:, c5:])

    e01 = (jnp.arange(LANES)[:, None] == (jnp.arange(DINNER)[None, :] // HEADDIM)).astype(BF16)
    d_x = jnp.repeat(ssm_D[l].astype(F32), HEADDIM).reshape(1, DINNER)

    x1, h = _ffn(x, _row(ffn1_pre_norm[l]), _row(ffn1_post_norm[l]), _row(mix_pre_norm[l]),
                 bf(ffn1_w_gate[l]), bf(ffn1_w_up[l]), bf(ffn1_w_down[l]), emit_h=True)

    z = _mm(_mm_plain_kernel, h, [w_z], [], DINNER, F32, "proj_z")
    xbc = _mm(_mm_plain_kernel, h, [w_xbc], [], CONV_DIM, F32, "proj_xbc")
    dt = _mm(_mm_dt_kernel, h, [w_dt], [_pad_lanes(ssm_dt_bias[l])], LANES, F32, "proj_dt")
    u = _mm(_mm_glu_kernel, h, [w_ga, w_gb], [], CC_DIM, F32, "proj_glu")

    ssm_args = (e01, ssm_conv_w[l].astype(F32), _row(ssm_conv_b[l]), _pad_lanes(ssm_A_log[l]), d_x,
                _row(ssm_norm[l]))
    yn_p, ssm_p = _ssd_prompt(xbc, dt, z, *ssm_args)
    xbc_s = xbc[N_PROMPT:].reshape(DEC_BATCH, DEC_SEQ, CONV_DIM)
    yn_s, ssm_s = _ssd_sample(xbc_s, state_ssm_conv[l], dt, z,
                              state_ssm[l].reshape(DEC_BATCH, DINNER, STATE), *ssm_args)

    cc_args = (cc_conv_w[l].astype(F32), _row(cc_conv_b[l]), _row(cc_ln_g[l]), _row(cc_ln_b[l]))
    tq = 256
    uc_p = _cc_prompt(u[:N_PROMPT].reshape(N_PROMPT // tq, tq, CC_DIM), *cc_args, tq=tq)
    u_s = u[N_PROMPT:].reshape(DEC_BATCH, DEC_SEQ, CC_DIM)
    uc_s = _cc_sample(u_s, state_cc_conv[l], *cc_args, n_seg=16)

    yn = jnp.concatenate([yn_p, yn_s], axis=0)
    uc = jnp.concatenate([uc_p, uc_s], axis=0)
    mg = _merge(h, yn, uc, w_gs, w_gc, bf(w_ssd_out[l]), bf(w_cc_out[l]))
    x2 = _oproj(mg, x1, bf(w_o[l]), _row(mix_post_norm[l]))

    (x3,) = _ffn(x2, _row(ffn2_pre_norm[l]), _row(ffn2_post_norm[l]), _row(ffn2_post_norm[l]),
                 bf(ffn2_w_gate[l]), bf(ffn2_w_up[l]), bf(ffn2_w_down[l]), emit_h=False)

    y_prompt = x3[:N_PROMPT].reshape(BATCH, SEQ, D_MODEL)
    y_sample = x3[N_PROMPT:].reshape(DEC_BATCH, DEC_SEQ, D_MODEL)
    new_ssm_p = ssm_p.reshape(1, BATCH, HEADS, HEADDIM, STATE)
    new_ssm_s = ssm_s.reshape(1, DEC_BATCH, HEADS, HEADDIM, STATE)
    xbc_p3 = xbc[:N_PROMPT].reshape(BATCH, SEQ, CONV_DIM)
    new_sconv_p = xbc_p3[:, SEQ - (SSM_K - 1):][None]
    new_sconv_s = xbc_s[:, DEC_SEQ - (SSM_K - 1):][None]
    u_p3 = u[:N_PROMPT].reshape(BATCH, SEQ, CC_DIM)
    new_cc_p = u_p3[:, SEQ - (CC_K - 1):][None]
    new_cc_s = jnp.concatenate([state_cc_conv[l][:, DEC_SEQ:], u_s], axis=1)[None]
    return (y_prompt, y_sample, new_ssm_p, new_sconv_p, new_cc_p, new_ssm_s, new_sconv_s, new_cc_s)
```

```python
import functools

import jax
import jax.numpy as jnp
from jax import lax
from jax.experimental import pallas as pl
from jax.experimental.pallas import tpu as pltpu

F32 = jnp.float32
BF16 = jnp.bfloat16

D_MODEL = 2048
D_FF = 5632
BATCH, SEQ = 4, 2048
DEC_BATCH, DEC_SEQ = 128, 8
N_PROMPT = BATCH * SEQ
N_SAMPLE = DEC_BATCH * DEC_SEQ
N_TOK = N_PROMPT + N_SAMPLE
HEADS, HEADDIM, GROUPS, STATE = 32, 64, 4, 128
HPG = HEADS // GROUPS
DINNER = HEADS * HEADDIM
GW = HPG * HEADDIM
SSM_K = 4
CONV_DIM = DINNER + 2 * GROUPS * STATE
CC_DIM = D_MODEL // 2
CC_K = 31
EPS = 1e-6

LANES = 128
SUBLANES = 8
CHUNK = 128
SEGS = CHUNK // DEC_SEQ
SEQ_PER_STEP = 4
VMEM_LIMIT = 48 * 1024 * 1024
FFN_VMEM_LIMIT = 58 * 1024 * 1024


def _cparams(sem, vmem=VMEM_LIMIT):
    return pltpu.CompilerParams(dimension_semantics=sem, vmem_limit_bytes=vmem)


def _rms(x, g):
    ms = jnp.mean(x * x, axis=-1, keepdims=True)
    return (x * lax.rsqrt(ms + EPS)) * g


def _dot(a, b):
    return jnp.dot(a, b, preferred_element_type=F32)


def _split3(x):
    hi = x.astype(BF16)
    r = x - hi.astype(F32)
    mid = r.astype(BF16)
    lo = (r - mid.astype(F32)).astype(BF16)
    return hi, mid, lo


def _dot01_rhs(x, m01):
    hi, mid, lo = _split3(x)
    return _dot(hi, m01) + _dot(mid, m01) + _dot(lo, m01)


def _dot01_lhs(m01, x):
    hi, mid, lo = _split3(x)
    return _dot(m01, hi) + _dot(m01, mid) + _dot(m01, lo)


def _ffn_kernel(*refs, n_x, n_y, emit_h, npt):
    x_refs = refs[:n_x]
    pre_ref, post_ref, nxt_ref, wg_ref, wu_ref, wd_ref = refs[n_x:n_x + 6]
    y_refs = refs[n_x + 6:n_x + 6 + n_y]
    rest = refs[n_x + 6 + n_y:]
    if emit_h:
        ho_ref, h_scr, acc_scr = rest
    else:
        h_scr, acc_scr = rest
    i = pl.program_id(0)
    j = pl.program_id(1)

    def load_x():
        if n_x == 1:
            return x_refs[0][...]
        return jnp.where(i < npt, x_refs[0][...], x_refs[1][...])

    @pl.when(j == 0)
    def _():
        h_scr[...] = _rms(load_x(), pre_ref[...]).astype(BF16)
        acc_scr[...] = jnp.zeros_like(acc_scr)

    h = h_scr[...]
    g = _dot(h, wg_ref[...])
    u = _dot(h, wu_ref[...])
    a = (jax.nn.silu(g) * u).astype(BF16)
    acc_scr[...] += _dot(a, wd_ref[...])

    @pl.when(j == pl.num_programs(1) - 1)
    def _():
        xn = load_x() + 0.5 * _rms(acc_scr[...], post_ref[...])
        if n_y == 1:
            y_refs[0][...] = xn
        else:
            @pl.when(i < npt)
            def _():
                y_refs[0][...] = xn

            @pl.when(i >= npt)
            def _():
                y_refs[1][...] = xn
        if emit_h:
            ho_ref[...] = _rms(xn, nxt_ref[...]).astype(BF16)


def _ffn(xs, pre, post, nxt, wg, wu, wd, *, n_y, emit_h, tm=512, tf=512):
    npt = N_PROMPT // tm
    grid = (N_TOK // tm, D_FF // tf)
    row = lambda i, j: (i, 0)
    first = lambda i, j: (jnp.minimum(i, npt - 1), 0)
    second = lambda i, j: (jnp.maximum(i - npt, 0), 0)
    vec = pl.BlockSpec((1, D_MODEL), lambda i, j: (0, 0))
    tile = lambda imap: pl.BlockSpec((tm, D_MODEL), imap)
    x_specs = [tile(row)] if len(xs) == 1 else [tile(first), tile(second)]
    if n_y == 1:
        out_shape = [jax.ShapeDtypeStruct((N_TOK, D_MODEL), F32)]
        out_specs = [tile(row)]
    else:
        out_shape = [jax.ShapeDtypeStruct((N_PROMPT, D_MODEL), F32),
                     jax.ShapeDtypeStruct((N_SAMPLE, D_MODEL), F32)]
        out_specs = [tile(first), tile(second)]
    if emit_h:
        out_shape.append(jax.ShapeDtypeStruct((N_TOK, D_MODEL), BF16))
        out_specs.append(tile(row))
    return pl.pallas_call(
        functools.partial(_ffn_kernel, n_x=len(xs), n_y=n_y, emit_h=emit_h, npt=npt),
        grid=grid,
        in_specs=x_specs + [
            vec, vec, vec,
            pl.BlockSpec((D_MODEL, tf), lambda i, j: (0, j)),
            pl.BlockSpec((D_MODEL, tf), lambda i, j: (0, j)),
            pl.BlockSpec((tf, D_MODEL), lambda i, j: (j, 0)),
        ],
        out_specs=out_specs,
        out_shape=out_shape,
        scratch_shapes=[pltpu.VMEM((tm, D_MODEL), BF16), pltpu.VMEM((tm, D_MODEL), F32)],
        compiler_params=_cparams(("arbitrary", "arbitrary"), FFN_VMEM_LIMIT),
        name="ffn_in" if emit_h else "ffn_out",
    )(*xs, pre, post, nxt, wg, wu, wd)


def _mm_plain_kernel(h_ref, w_ref, o_ref):
    o_ref[...] = _dot(h_ref[...], w_ref[...]).astype(o_ref.dtype)


def _mm_dt_kernel(h_ref, w_ref, b_ref, o_ref):
    dt = jax.nn.softplus(_dot(h_ref[...], w_ref[...]) + b_ref[...])
    lane = lax.broadcasted_iota(jnp.int32, dt.shape, 1)
    o_ref[...] = jnp.where(lane < HEADS, dt, 0.0)


def _mm_glu_kernel(h_ref, wa_ref, wb_ref, o_ref):
    h = h_ref[...]
    o_ref[...] = _dot(h, wa_ref[...]) * jax.nn.sigmoid(_dot(h, wb_ref[...]))


def _mm(kernel, h, ws, extras, n_out, out_dtype, name, *, tm=1024, tn=512):
    m, k = h.shape
    tn = min(tn, n_out)
    grid = (m // tm, n_out // tn)
    in_specs = [pl.BlockSpec((tm, k), lambda i, j: (i, 0))]
    for _, col in ws:
        assert col % tn == 0
        in_specs.append(pl.BlockSpec((k, tn), functools.partial(lambda i, j, o: (0, o + j), o=col // tn)))
    in_specs += [pl.BlockSpec((1, tn), lambda i, j: (0, j)) for _ in extras]
    return pl.pallas_call(
        kernel,
        grid=grid,
        in_specs=in_specs,
        out_specs=pl.BlockSpec((tm, tn), lambda i, j: (i, j)),
        out_shape=jax.ShapeDtypeStruct((m, n_out), out_dtype),
        compiler_params=_cparams(("parallel", "arbitrary")),
        name=name,
    )(h, *[w for w, _ in ws], *extras)


def _block_masks(seg_len):
    t = lax.broadcasted_iota(jnp.int32, (CHUNK, CHUNK), 0)
    s = lax.broadcasted_iota(jnp.int32, (CHUNK, CHUNK), 1)
    if seg_len == CHUNK:
        same = t >= 0
    else:
        sh = seg_len.bit_length() - 1
        same = (t >> sh) == (s >> sh)
    causal = jnp.logical_and(same, s <= t)
    return same, causal


CONV_STRIP = 512


def _conv4_silu(load_strip, w8_ref, cb_ref):
    strips = []
    for c0 in range(0, CONV_DIM, CONV_STRIP):
        cols = slice(c0, c0 + CONV_STRIP)
        full = load_strip(cols)
        n = full.shape[-2] - SUBLANES
        acc = None
        for k in range(SSM_K):
            win = full[..., 5 + k:5 + k + n, :].reshape(CHUNK // SUBLANES, SUBLANES, CONV_STRIP)
            t = w8_ref[k, :, cols][None] * win
            acc = t if acc is None else acc + t
        strips.append(jax.nn.silu(acc.reshape(CHUNK, CONV_STRIP) + cb_ref[:, cols]))
    return strips


def _ssd_intra(strips, dt, alog_row, e01, seg_len):
    same, causal = _block_masks(seg_len)
    n_x = DINNER // CONV_STRIP
    xs = jnp.concatenate(strips[:n_x], axis=1)
    bm = strips[n_x]
    cm = strips[n_x + 1]
    a_row = -jnp.exp(alog_row)
    dta = dt * a_row
    causal01 = jnp.where(causal, 1.0, 0.0).astype(BF16)
    same01 = jnp.where(same, 1.0, 0.0).astype(BF16)
    cum = _dot01_lhs(causal01, dta)
    cum_end = _dot01_lhs(same01, dta)
    cum_t = cum.T
    dt_t = dt.T
    ecum_x = _dot01_rhs(jnp.exp(cum), e01)
    coef_x = _dot01_rhs(jnp.exp(cum_end - cum) * dt, e01)

    lane = lax.broadcasted_iota(jnp.int32, (CHUNK, LANES), 1)
    first_head = lane < HEADDIM
    ys = []
    for g in range(GROUPS):
        cg = cm[:, g * STATE:(g + 1) * STATE].astype(BF16)
        bg = bm[:, g * STATE:(g + 1) * STATE].astype(BF16)
        cb = lax.dot_general(cg, bg, (((1,), (1,)), ((), ())), preferred_element_type=F32)
        for p in range(HPG // 2):
            ws = []
            for e in (g * HPG + 2 * p, g * HPG + 2 * p + 1):
                seg = cum[:, e:e + 1] - cum_t[e:e + 1, :]
                dec = jnp.exp(jnp.where(causal, seg, -jnp.inf))
                ws.append(((cb * dec) * dt_t[e:e + 1, :]).astype(BF16))
            w2 = jnp.concatenate(ws, axis=1)
            col = (g * HPG + 2 * p) * HEADDIM
            x2 = xs[:, col:col + LANES]
            r2 = jnp.concatenate([jnp.where(first_head, x2, 0.0),
                                  jnp.where(first_head, 0.0, x2)], axis=0).astype(BF16)
            ys.append(_dot(w2, r2))
    y_diag = jnp.concatenate(ys, axis=1)
    return xs, cm, bm, y_diag, coef_x, ecum_x


def _gated_norm(y, z, nrm_row):
    return _rms(y * jax.nn.silu(z), nrm_row).astype(BF16)


def _ssd_prompt_kernel(xbc_ref, dt_ref, z_ref, e_ref, w8_ref, cb_ref, alog_ref, dx_ref, nrm_ref,
                       yn_ref, hout_ref, xp_ref, ht_ref):
    c = pl.program_id(1)

    @pl.when(c == 0)
    def _():
        xp_ref[0:SUBLANES, :] = jnp.zeros((SUBLANES, CONV_DIM), F32)
        ht_ref[...] = jnp.zeros_like(ht_ref)

    xp_ref[SUBLANES:SUBLANES + CHUNK, :] = xbc_ref[...]
    strips = _conv4_silu(lambda cols: xp_ref[:, cols], w8_ref, cb_ref)
    xp_ref[0:SUBLANES, :] = xbc_ref[CHUNK - SUBLANES:CHUNK, :]

    xs, cm, bm, y_diag, coef_x, ecum_x = _ssd_intra(strips, dt_ref[...], alog_ref[...], e_ref[...], CHUNK)
    xcoef = (xs * coef_x).astype(BF16)
    dec_row = ecum_x[CHUNK - 1:CHUNK, :]
    y_off = []
    for g in range(GROUPS):
        gs = slice(g * GW, (g + 1) * GW)
        ht_g = ht_ref[:, gs]
        cg = cm[:, g * STATE:(g + 1) * STATE].astype(BF16)
        y_off.append(_dot(cg, ht_g.astype(BF16)))
        bg_t = bm[:, g * STATE:(g + 1) * STATE].T.astype(BF16)
        ht_ref[:, gs] = ht_g * dec_row[:, gs] + _dot(bg_t, xcoef[:, gs])
    y = y_diag + jnp.concatenate(y_off, axis=1) * ecum_x + dx_ref[...] * xs
    yn_ref[...] = _gated_norm(y, z_ref[...], nrm_ref[...])

    @pl.when(c == pl.num_programs(1) - 1)
    def _():
        hout_ref[...] = ht_ref[...].T


def _ssm_param_specs(const):
    return [
        pl.BlockSpec((LANES, DINNER), const),
        pl.BlockSpec((SSM_K, SUBLANES, CONV_DIM), lambda *_: (0, 0, 0)),
        pl.BlockSpec((1, CONV_DIM), const),
        pl.BlockSpec((1, LANES), const),
        pl.BlockSpec((1, DINNER), const),
        pl.BlockSpec((1, DINNER), const),
    ]


def _ssd_prompt(xbc, dt, z, *params):
    nc = SEQ // CHUNK
    blk = lambda b, c: (b * nc + c, 0)
    const = lambda b, c: (0, 0)
    return pl.pallas_call(
        _ssd_prompt_kernel,
        grid=(BATCH, nc),
        in_specs=[
            pl.BlockSpec((CHUNK, CONV_DIM), blk),
            pl.BlockSpec((CHUNK, LANES), blk),
            pl.BlockSpec((CHUNK, DINNER), blk),
        ] + _ssm_param_specs(const),
        out_specs=[
            pl.BlockSpec((CHUNK, DINNER), blk),
            pl.BlockSpec((None, DINNER, STATE), lambda b, c: (b, 0, 0)),
        ],
        out_shape=[
            jax.ShapeDtypeStruct((N_TOK, DINNER), BF16),
            jax.ShapeDtypeStruct((BATCH, DINNER, STATE), F32),
        ],
        scratch_shapes=[pltpu.VMEM((CHUNK + SUBLANES, CONV_DIM), F32), pltpu.VMEM((STATE, DINNER), F32)],
        compiler_params=_cparams(("parallel", "arbitrary")),
        name="ssd_prompt",
    )(xbc, dt, z, *params)


def _ssd_sample_kernel(yn_any, xbc_ref, cst_ref, dt_ref, z_ref, h0_ref, e_ref, w8_ref, cb_ref,
                       alog_ref, dx_ref, nrm_ref, yn_ref, hout_ref,
                       xp_ref, cm_ref, bt_ref, xcoef_ref, ecx_ref, y_ref):
    del yn_any
    j = pl.program_id(1)

    @pl.when(j == 0)
    def _():
        xp_ref[:, SUBLANES - (SSM_K - 1):SUBLANES, :] = cst_ref[...]
        xp_ref[:, SUBLANES:SUBLANES + DEC_SEQ, :] = xbc_ref[...]
        strips = _conv4_silu(lambda cols: xp_ref[:, :, cols], w8_ref, cb_ref)
        xs, cm, bm, y_diag, coef_x, ecum_x = _ssd_intra(
            strips, dt_ref[...], alog_ref[...], e_ref[...], DEC_SEQ)
        cm_ref[...] = cm
        for g in range(GROUPS):
            bt_ref[g] = bm[:, g * STATE:(g + 1) * STATE].T
        xcoef_ref[...] = (xs * coef_x).astype(BF16)
        ecx_ref[...] = ecum_x
        y_ref[...] = y_diag + dx_ref[...] * xs

    col = lax.broadcasted_iota(jnp.int32, (STATE, CHUNK), 1)
    base = j * (SEQ_PER_STEP * DEC_SEQ)
    for p in range(SEQ_PER_STEP):
        r0 = pl.multiple_of(base + p * DEC_SEQ, DEC_SEQ)
        rows = pl.ds(r0, DEC_SEQ)
        ht = h0_ref[p].T
        ht_b = ht.astype(BF16)
        ecx = ecx_ref[rows, :]
        cj = cm_ref[rows, :].astype(BF16)
        mine = jnp.logical_and(col >= r0, col < r0 + DEC_SEQ)
        y_off, new = [], []
        for g in range(GROUPS):
            gs = slice(g * GW, (g + 1) * GW)
            y_off.append(_dot(cj[:, g * STATE:(g + 1) * STATE], ht_b[:, gs]))
            bg_t = jnp.where(mine, bt_ref[g], 0.0).astype(BF16)
            new.append(_dot(bg_t, xcoef_ref[:, gs]))
        y_ref[rows, :] += jnp.concatenate(y_off, axis=1) * ecx
        ht_new = ht * ecx[DEC_SEQ - 1:DEC_SEQ, :] + jnp.concatenate(new, axis=1)
        hout_ref[p] = ht_new.T

    @pl.when(j == pl.num_programs(1) - 1)
    def _():
        yn_ref[...] = _gated_norm(y_ref[...], z_ref[...], nrm_ref[...])


def _ssd_sample(yn_all, xbc3, cst, dt, z, h0, *params):
    nb = DEC_BATCH // SEGS
    off = N_PROMPT // CHUNK
    const = lambda i, j: (0, 0)
    tok = lambda i, j: (off + i, 0)
    seq = lambda i, j: (i * (SEGS // SEQ_PER_STEP) + j, 0, 0)
    return pl.pallas_call(
        _ssd_sample_kernel,
        grid=(nb, SEGS // SEQ_PER_STEP),
        in_specs=[
            pl.BlockSpec(memory_space=pl.ANY),
            pl.BlockSpec((SEGS, DEC_SEQ, CONV_DIM), lambda i, j: (off + i, 0, 0)),
            pl.BlockSpec((SEGS, SSM_K - 1, CONV_DIM), lambda i, j: (i, 0, 0)),
            pl.BlockSpec((CHUNK, LANES), tok),
            pl.BlockSpec((CHUNK, DINNER), tok),
            pl.BlockSpec((SEQ_PER_STEP, DINNER, STATE), seq),
        ] + _ssm_param_specs(const),
        out_specs=[
            pl.BlockSpec((CHUNK, DINNER), tok),
            pl.BlockSpec((SEQ_PER_STEP, DINNER, STATE), seq),
        ],
        out_shape=[
            jax.ShapeDtypeStruct((N_TOK, DINNER), BF16),
            jax.ShapeDtypeStruct((DEC_BATCH, DINNER, STATE), F32),
        ],
        scratch_shapes=[
            pltpu.VMEM((SEGS, SUBLANES + DEC_SEQ, CONV_DIM), F32),
            pltpu.VMEM((CHUNK, GROUPS * STATE), F32),
            pltpu.VMEM((GROUPS, STATE, CHUNK), F32),
            pltpu.VMEM((CHUNK, DINNER), BF16),
            pltpu.VMEM((CHUNK, DINNER), F32),
            pltpu.VMEM((CHUNK, DINNER), F32),
        ],
        input_output_aliases={0: 0},
        compiler_params=_cparams(("parallel", "arbitrary")),
        name="ssd_sample",
    )(yn_all, xbc3, cst, dt, z, h0, *params)


CC_PAD = 32
CC_FIRST = CC_PAD - (CC_K - 1)
CC_LN_ROWS = 32


def _cc_kernel(*refs, n_seg, seg_len, carry, aliased):
    refs = list(refs)
    if aliased:
        refs.pop(0)
    u_ref = refs.pop(0)
    st_ref = None if carry else refs.pop(0)
    w8_ref, b_ref, g_ref, be_ref, o_ref, xp_ref, res_ref = refs
    if carry:
        @pl.when(pl.program_id(1) == 0)
        def _():
            xp_ref[:, 0:CC_PAD, :] = jnp.zeros((n_seg, CC_PAD, CC_DIM), F32)
    else:
        xp_ref[:, CC_FIRST:CC_PAD, :] = st_ref[...]
    xp_ref[:, CC_PAD:CC_PAD + seg_len, :] = u_ref[...]

    rb = min(seg_len, 64)
    n_shift = rb + CC_PAD - SUBLANES
    for s in range(n_seg):
        for t0 in range(0, seg_len, rb):
            for c0 in range(0, CC_DIM, LANES):
                cols = slice(c0, c0 + LANES)
                full = xp_ref[s, t0:t0 + rb + CC_PAD, cols]
                acc = None
                for b in range(SUBLANES):
                    xb = full if b == 0 else full[b:b + n_shift, :]
                    for a in range(CC_PAD // SUBLANES + 1):
                        k = SUBLANES * a + b - CC_FIRST
                        if k < 0 or k >= CC_K:
                            continue
                        win = xb[SUBLANES * a:SUBLANES * a + rb, :].reshape(rb // SUBLANES, SUBLANES, LANES)
                        t = w8_ref[k, :, cols][None] * win
                        acc = t if acc is None else acc + t
                row = s * seg_len + t0
                res_ref[row:row + rb, cols] = acc.reshape(rb, LANES) + b_ref[:, cols]

    for r0 in range(0, n_seg * seg_len, CC_LN_ROWS):
        v = res_ref[r0:r0 + CC_LN_ROWS, :]
        mu = jnp.mean(v, axis=-1, keepdims=True)
        d = v - mu
        var = jnp.mean(d * d, axis=-1, keepdims=True)
        yv = (d * lax.rsqrt(var + EPS)) * g_ref[...] + be_ref[...]
        o_ref[r0:r0 + CC_LN_ROWS, :] = jax.nn.silu(yv).astype(BF16)
    if carry:
        xp_ref[:, 0:CC_PAD, :] = xp_ref[:, seg_len:seg_len + CC_PAD, :]


def _cc_param_specs(const):
    return [
        pl.BlockSpec((CC_K, SUBLANES, CC_DIM), lambda *_: (0, 0, 0)),
        pl.BlockSpec((1, CC_DIM), const),
        pl.BlockSpec((1, CC_DIM), const),
        pl.BlockSpec((1, CC_DIM), const),
    ]


def _cc_prompt(u3, *params, tq):
    nc = SEQ // tq
    const = lambda i, c: (0, 0)
    return pl.pallas_call(
        functools.partial(_cc_kernel, n_seg=1, seg_len=tq, carry=True, aliased=False),
        grid=(BATCH, nc),
        in_specs=[pl.BlockSpec((1, tq, CC_DIM), lambda i, c: (i * nc + c, 0, 0))] + _cc_param_specs(const),
        out_specs=pl.BlockSpec((tq, CC_DIM), lambda i, c: (i * nc + c, 0)),
        out_shape=jax.ShapeDtypeStruct((N_TOK, CC_DIM), BF16),
        scratch_shapes=[pltpu.VMEM((1, CC_PAD + tq, CC_DIM), F32), pltpu.VMEM((tq, CC_DIM), F32)],
        compiler_params=_cparams(("parallel", "arbitrary")),
        name="cc_prompt",
    )(u3, *params)


def _cc_sample(uc_all, u3, st, *params, n_seg):
    const = lambda i: (0, 0)
    rows = n_seg * DEC_SEQ
    off = N_PROMPT // rows
    return pl.pallas_call(
        functools.partial(_cc_kernel, n_seg=n_seg, seg_len=DEC_SEQ, carry=False, aliased=True),
        grid=(DEC_BATCH // n_seg,),
        in_specs=[
            pl.BlockSpec(memory_space=pl.ANY),
            pl.BlockSpec((n_seg, DEC_SEQ, CC_DIM), lambda i: (off + i, 0, 0)),
            pl.BlockSpec((n_seg, CC_K - 1, CC_DIM), lambda i: (i, 0, 0)),
        ] + _cc_param_specs(const),
        out_specs=pl.BlockSpec((rows, CC_DIM), lambda i: (off + i, 0)),
        out_shape=jax.ShapeDtypeStruct((N_TOK, CC_DIM), BF16),
        scratch_shapes=[pltpu.VMEM((n_seg, CC_PAD + DEC_SEQ, CC_DIM), F32),
                        pltpu.VMEM((rows, CC_DIM), F32)],
        input_output_aliases={0: 0},
        compiler_params=_cparams(("parallel",)),
        name="cc_sample",
    )(uc_all, u3, st, *params)


def _merge_kernel(h_ref, yn_ref, uc_ref, wgs_ref, wgc_ref, wso_ref, wco_ref, o_ref):
    h = h_ref[...]
    g_ssd = jax.nn.sigmoid(_dot(h, wgs_ref[...]))
    g_cc = jax.nn.sigmoid(_dot(h, wgc_ref[...]))
    ssd_out = _dot(yn_ref[...], wso_ref[...])
    cc_out = _dot(uc_ref[...], wco_ref[...])
    o_ref[...] = (g_ssd * ssd_out + g_cc * cc_out).astype(BF16)


def _merge(h, yn, uc, wgs, wgc, wso, wco, *, tm=1024, tn=512):
    m = h.shape[0]
    row = lambda i, j: (i, 0)
    colw = lambda i, j: (0, j)
    return pl.pallas_call(
        _merge_kernel,
        grid=(m // tm, D_MODEL // tn),
        in_specs=[
            pl.BlockSpec((tm, D_MODEL), row),
            pl.BlockSpec((tm, DINNER), row),
            pl.BlockSpec((tm, CC_DIM), row),
            pl.BlockSpec((D_MODEL, tn), colw),
            pl.BlockSpec((D_MODEL, tn), colw),
            pl.BlockSpec((DINNER, tn), colw),
            pl.BlockSpec((CC_DIM, tn), colw),
        ],
        out_specs=pl.BlockSpec((tm, tn), lambda i, j: (i, j)),
        out_shape=jax.ShapeDtypeStruct((m, D_MODEL), BF16),
        compiler_params=_cparams(("parallel", "arbitrary")),
        name="merge",
    )(h, yn, uc, wgs, wgc, wso, wco)


def _oproj_kernel(m_ref, x_ref, w_ref, nrm_ref, o_ref):
    o_ref[...] = x_ref[...] + _rms(_dot(m_ref[...], w_ref[...]), nrm_ref[...])


def _oproj(mg, x, w, nrm, *, tm=512):
    m = x.shape[0]
    row = lambda i: (i, 0)
    const = lambda i: (0, 0)
    return pl.pallas_call(
        _oproj_kernel,
        grid=(m // tm,),
        in_specs=[
            pl.BlockSpec((tm, D_MODEL), row),
            pl.BlockSpec((tm, D_MODEL), row),
            pl.BlockSpec((D_MODEL, D_MODEL), const),
            pl.BlockSpec((1, D_MODEL), const),
        ],
        out_specs=pl.BlockSpec((tm, D_MODEL), row),
        out_shape=jax.ShapeDtypeStruct((m, D_MODEL), F32),
        compiler_params=_cparams(("parallel",)),
        name="oproj",
    )(mg, x, w, nrm)


def _row(v):
    return v.reshape(1, -1).astype(F32)


def _pad_lanes(v):
    return jnp.pad(v.astype(F32), (0, LANES - v.shape[0])).reshape(1, LANES)


def _taps_on_sublanes(w):
    return jnp.broadcast_to(w.astype(F32)[:, None, :], (w.shape[0], SUBLANES, w.shape[1]))


def kernel(x_prompt, x_sample, state_ssm, state_ssm_conv, state_cc_conv, ffn1_pre_norm, ffn1_post_norm, ffn1_w_gate, ffn1_w_up, ffn1_w_down, mix_pre_norm, mix_post_norm, w_in, ssm_conv_w, ssm_conv_b, ssm_dt_bias, ssm_A_log, ssm_D, ssm_norm, w_ssd_out, cc_conv_w, cc_conv_b, cc_ln_g, cc_ln_b, w_cc_out, w_o, ffn2_pre_norm, ffn2_post_norm, ffn2_w_gate, ffn2_w_up, ffn2_w_down):
    l = 0
    bf = lambda w: w.astype(BF16)

    wb = lax.optimization_barrier(bf(w_in[l]))
    c_xbc = DINNER
    c_dt = c_xbc + CONV_DIM
    c2 = c_dt + HEADS
    c3 = c2 + CC_DIM
    c4 = c3 + CC_DIM
    c5 = c4 + D_MODEL
    w_ga, w_gb = wb[:, c2:c3], wb[:, c3:c4]
    w_gs, w_gc = wb[:, c4:c5], wb[:, c5:]

    e01 = (jnp.arange(LANES)[:, None] == (jnp.arange(DINNER)[None, :] // HEADDIM)).astype(BF16)
    d_x = jnp.repeat(ssm_D[l].astype(F32), HEADDIM).reshape(1, DINNER)

    xs = (x_prompt.reshape(N_PROMPT, D_MODEL), x_sample.reshape(N_SAMPLE, D_MODEL))
    x1, h = _ffn(xs, _row(ffn1_pre_norm[l]), _row(ffn1_post_norm[l]), _row(mix_pre_norm[l]),
                 bf(ffn1_w_gate[l]), bf(ffn1_w_up[l]), bf(ffn1_w_down[l]), n_y=1, emit_h=True)

    z = _mm(_mm_plain_kernel, h, [(wb, 0)], [], DINNER, F32, "proj_z")
    xbc = _mm(_mm_plain_kernel, h, [(wb, c_xbc)], [], CONV_DIM, F32, "proj_xbc")
    dt = _mm(_mm_dt_kernel, h, [(wb, c_dt)], [_pad_lanes(ssm_dt_bias[l])], LANES, F32, "proj_dt")
    u = _mm(_mm_glu_kernel, h, [(w_ga, 0), (w_gb, 0)], [], CC_DIM, F32, "proj_glu")

    ssm_params = (e01, _taps_on_sublanes(ssm_conv_w[l]), _row(ssm_conv_b[l]), _pad_lanes(ssm_A_log[l]),
                  d_x, _row(ssm_norm[l]))
    yn, ssm_p = _ssd_prompt(xbc, dt, z, *ssm_params)
    yn, ssm_s = _ssd_sample(yn, xbc.reshape(N_TOK // DEC_SEQ, DEC_SEQ, CONV_DIM), state_ssm_conv[l], dt, z,
                            state_ssm[l].reshape(DEC_BATCH, DINNER, STATE), *ssm_params)

    cc_params = (_taps_on_sublanes(cc_conv_w[l]), _row(cc_conv_b[l]), _row(cc_ln_g[l]), _row(cc_ln_b[l]))
    tq = 256
    uc = _cc_prompt(u.reshape(N_TOK // tq, tq, CC_DIM), *cc_params, tq=tq)
    uc = _cc_sample(uc, u.reshape(N_TOK // DEC_SEQ, DEC_SEQ, CC_DIM), state_cc_conv[l], *cc_params, n_seg=16)

    mg = _merge(h, yn, uc, w_gs, w_gc, bf(w_ssd_out[l]), bf(w_cc_out[l]))
    x2 = _oproj(mg, x1, bf(w_o[l]), _row(mix_post_norm[l]))

    y_p, y_s = _ffn((x2,), _row(ffn2_pre_norm[l]), _row(ffn2_post_norm[l]), _row(ffn2_post_norm[l]),
                    bf(ffn2_w_gate[l]), bf(ffn2_w_up[l]), bf(ffn2_w_down[l]), n_y=2, emit_h=False)

    y_prompt = y_p.reshape(BATCH, SEQ, D_MODEL)
    y_sample = y_s.reshape(DEC_BATCH, DEC_SEQ, D_MODEL)
    new_ssm_p = ssm_p.reshape(1, BATCH, HEADS, HEADDIM, STATE)
    new_ssm_s = ssm_s.reshape(1, DEC_BATCH, HEADS, HEADDIM, STATE)
    xbc_p3 = xbc[:N_PROMPT].reshape(BATCH, SEQ, CONV_DIM)
    new_sconv_p = xbc_p3[:, SEQ - (SSM_K - 1):][None]
    xbc_s3 = xbc[N_PROMPT:].reshape(DEC_BATCH, DEC_SEQ, CONV_DIM)
    new_sconv_s = xbc_s3[:, DEC_SEQ - (SSM_K - 1):][None]
    u_p3 = u[:N_PROMPT].reshape(BATCH, SEQ, CC_DIM)
    new_cc_p = u_p3[:, SEQ - (CC_K - 1):][None]
    u_s3 = u[N_PROMPT:].reshape(DEC_BATCH, DEC_SEQ, CC_DIM)
    new_cc_s = jnp.concatenate([state_cc_conv[l][:, DEC_SEQ:], u_s3], axis=1)[None]
    return (y_prompt, y_sample, new_ssm_p, new_sconv_p, new_cc_p, new_ssm_s, new_sconv_s, new_cc_s)
```

```python
import functools

import jax
import jax.numpy as jnp
from jax import lax
from jax.experimental import pallas as pl
from jax.experimental.pallas import tpu as pltpu

F32 = jnp.float32
BF16 = jnp.bfloat16

D_MODEL = 2048
D_FF = 5632
BATCH, SEQ = 4, 2048
DEC_BATCH, DEC_SEQ = 128, 8
N_PROMPT = BATCH * SEQ
N_SAMPLE = DEC_BATCH * DEC_SEQ
N_TOK = N_PROMPT + N_SAMPLE
HEADS, HEADDIM, GROUPS, STATE = 32, 64, 4, 128
HPG = HEADS // GROUPS
DINNER = HEADS * HEADDIM
GW = HPG * HEADDIM
SSM_K = 4
CONV_DIM = DINNER + 2 * GROUPS * STATE
CC_DIM = D_MODEL // 2
CC_K = 31
EPS = 1e-6

LANES = 128
SUBLANES = 8
CHUNK = 128
SEGS = CHUNK // DEC_SEQ
SEQ_PER_STEP = 4
VMEM_LIMIT = 48 * 1024 * 1024
FFN_VMEM_LIMIT = 58 * 1024 * 1024


def _cparams(sem, vmem=VMEM_LIMIT):
    return pltpu.CompilerParams(dimension_semantics=sem, vmem_limit_bytes=vmem)


def _rms(x, g):
    ms = jnp.mean(x * x, axis=-1, keepdims=True)
    return (x * lax.rsqrt(ms + EPS)) * g


def _dot(a, b):
    return jnp.dot(a, b, preferred_element_type=F32)


def _split3(x):
    hi = x.astype(BF16)
    r = x - hi.astype(F32)
    mid = r.astype(BF16)
    lo = (r - mid.astype(F32)).astype(BF16)
    return hi, mid, lo


def _dot01_rhs(x, m01):
    hi, mid, lo = _split3(x)
    return _dot(hi, m01) + _dot(mid, m01) + _dot(lo, m01)


def _dot01_lhs(m01, x):
    hi, mid, lo = _split3(x)
    return _dot(m01, hi) + _dot(m01, mid) + _dot(m01, lo)


BF16_ROWS = 16
FFN_TM = 512
FFN_TF = 512


def _ffn_kernel(*refs, n_prev, n_cast, emit_h):
    refs = refs[n_prev:]
    x_ref, pre_ref, post_ref, nxt_ref, wg_ref, wu_ref, wd_ref = refs[:7]
    src_refs = refs[7:7 + n_cast]
    outs = refs[7 + n_cast:]
    y_ref = outs[0]
    ho_ref = outs[1] if emit_h else None
    dst_refs = outs[1 + emit_h:1 + emit_h + n_cast]
    h_scr, acc_scr = outs[1 + emit_h + n_cast:]
    j = pl.program_id(1)

    @pl.when(j == 0)
    def _():
        h_scr[...] = _rms(x_ref[...], pre_ref[...]).astype(BF16)
        acc_scr[...] = jnp.zeros_like(acc_scr)

    for src_ref, dst_ref in zip(src_refs, dst_refs):
        dst_ref[...] = src_ref[...].astype(BF16)

    h = h_scr[...]
    g = _dot(h, wg_ref[...])
    u = _dot(h, wu_ref[...])
    a = (jax.nn.silu(g) * u).astype(BF16)
    acc_scr[...] += _dot(a, wd_ref[...])

    @pl.when(j == pl.num_programs(1) - 1)
    def _():
        xn = x_ref[...] + 0.5 * _rms(acc_scr[...], post_ref[...])
        y_ref[...] = xn
        if emit_h:
            ho_ref[...] = _rms(xn, nxt_ref[...]).astype(BF16)


def _cast_chunk_rows(rows, n_steps):
    rc = BF16_ROWS
    while rows % rc or rows // rc > n_steps:
        rc += BF16_ROWS
    return rc


def _ffn(x, x_tile0, n_tiles, pre, post, nxt, wg, wu, wd, *, out_rows, out_tile0, emit_h,
         prev=(), casts=(), name, tm=FFN_TM, tf=FFN_TF):
    nj = D_FF // tf
    n_steps = n_tiles * nj
    vec = pl.BlockSpec((1, D_MODEL), lambda i, j: (0, 0))
    tile = lambda t0: pl.BlockSpec((tm, D_MODEL), functools.partial(lambda i, j, t: (t + i, 0), t=t0))
    out_shape = [jax.ShapeDtypeStruct((out_rows, D_MODEL), F32)]
    out_specs = [tile(out_tile0)]
    if emit_h:
        out_shape.append(jax.ShapeDtypeStruct((out_rows, D_MODEL), BF16))
        out_specs.append(tile(out_tile0))
    cast_specs = []
    for w in casts:
        rows, cols = w.shape
        rc = _cast_chunk_rows(rows, n_steps)
        spec = pl.BlockSpec((rc, cols), functools.partial(
            lambda i, j, last: (jnp.minimum(i * nj + j, last), 0), last=rows // rc - 1))
        cast_specs.append(spec)
        out_shape.append(jax.ShapeDtypeStruct((rows, cols), BF16))
        out_specs.append(spec)
    return pl.pallas_call(
        functools.partial(_ffn_kernel, n_prev=len(prev), n_cast=len(casts), emit_h=emit_h),
        grid=(n_tiles, nj),
        in_specs=[pl.BlockSpec(memory_space=pl.ANY)] * len(prev) + [
            tile(x_tile0), vec, vec, vec,
            pl.BlockSpec((D_MODEL, tf), lambda i, j: (0, j)),
            pl.BlockSpec((D_MODEL, tf), lambda i, j: (0, j)),
            pl.BlockSpec((tf, D_MODEL), lambda i, j: (j, 0)),
        ] + cast_specs,
        out_specs=out_specs,
        out_shape=out_shape,
        scratch_shapes=[pltpu.VMEM((tm, D_MODEL), BF16), pltpu.VMEM((tm, D_MODEL), F32)],
        input_output_aliases={k: k for k in range(len(prev))},
        compiler_params=_cparams(("arbitrary", "arbitrary"), FFN_VMEM_LIMIT),
        name=name,
    )(*prev, x, pre, post, nxt, wg, wu, wd, *casts)


WIDE_TN = 1024


def _mm_plain_kernel(h_ref, w_ref, o_ref):
    o_ref[...] = _dot(h_ref[...], w_ref[...]).astype(o_ref.dtype)


def _mm_dt_kernel(h_ref, w_ref, b_ref, o_ref):
    dt = jax.nn.softplus(_dot(h_ref[...], w_ref[...]) + b_ref[...])
    lane = lax.broadcasted_iota(jnp.int32, dt.shape, 1)
    o_ref[...] = jnp.where(lane < HEADS, dt, 0.0)


def _mm_glu_kernel(h_ref, wa_ref, wb_ref, o_ref):
    h = h_ref[...]
    o_ref[...] = _dot(h, wa_ref[...]) * jax.nn.sigmoid(_dot(h, wb_ref[...]))


def _mm(kernel, h, ws, extras, n_out, out_dtype, name, *, tm=1024, tn=512):
    m, k = h.shape
    tn = min(tn, n_out)
    grid = (m // tm, n_out // tn)
    in_specs = [pl.BlockSpec((tm, k), lambda i, j: (i, 0))]
    for _, col in ws:
        assert col % tn == 0
        in_specs.append(pl.BlockSpec((k, tn), functools.partial(lambda i, j, o: (0, o + j), o=col // tn)))
    in_specs += [pl.BlockSpec((1, tn), lambda i, j: (0, j)) for _ in extras]
    return pl.pallas_call(
        kernel,
        grid=grid,
        in_specs=in_specs,
        out_specs=pl.BlockSpec((tm, tn), lambda i, j: (i, j)),
        out_shape=jax.ShapeDtypeStruct((m, n_out), out_dtype),
        compiler_params=_cparams(("parallel", "arbitrary")),
        name=name,
    )(h, *[w for w, _ in ws], *extras)


def _block_masks(seg_len):
    t = lax.broadcasted_iota(jnp.int32, (CHUNK, CHUNK), 0)
    s = lax.broadcasted_iota(jnp.int32, (CHUNK, CHUNK), 1)
    if seg_len == CHUNK:
        same = t >= 0
    else:
        sh = seg_len.bit_length() - 1
        same = (t >> sh) == (s >> sh)
    causal = jnp.logical_and(same, s <= t)
    return same, causal


CONV_STRIP = 512


def _conv4_silu(load_strip, w8_ref, cb_ref):
    strips = []
    for c0 in range(0, CONV_DIM, CONV_STRIP):
        cols = slice(c0, c0 + CONV_STRIP)
        full = load_strip(cols)
        n = full.shape[-2] - SUBLANES
        acc = None
        for k in range(SSM_K):
            win = full[..., 5 + k:5 + k + n, :].reshape(CHUNK // SUBLANES, SUBLANES, CONV_STRIP)
            t = w8_ref[k, :, cols][None] * win
            acc = t if acc is None else acc + t
        strips.append(jax.nn.silu(acc.reshape(CHUNK, CONV_STRIP) + cb_ref[:, cols]))
    return strips


def _ssd_intra(strips, dt, alog_row, e01, seg_len):
    same, causal = _block_masks(seg_len)
    n_x = DINNER // CONV_STRIP
    xs = jnp.concatenate(strips[:n_x], axis=1)
    bm = strips[n_x]
    cm = strips[n_x + 1]
    a_row = -jnp.exp(alog_row)
    dta = dt * a_row
    causal01 = jnp.where(causal, 1.0, 0.0).astype(BF16)
    same01 = jnp.where(same, 1.0, 0.0).astype(BF16)
    cum = _dot01_lhs(causal01, dta)
    cum_end = _dot01_lhs(same01, dta)
    cum_t = cum.T
    dt_t = dt.T
    ecum_x = _dot01_rhs(jnp.exp(cum), e01)
    coef_x = _dot01_rhs(jnp.exp(cum_end - cum) * dt, e01)

    lane = lax.broadcasted_iota(jnp.int32, (CHUNK, LANES), 1)
    first_head = lane < HEADDIM
    ys = []
    for g in range(GROUPS):
        cg = cm[:, g * STATE:(g + 1) * STATE].astype(BF16)
        bg = bm[:, g * STATE:(g + 1) * STATE].astype(BF16)
        cb = lax.dot_general(cg, bg, (((1,), (1,)), ((), ())), preferred_element_type=F32)
        for p in range(HPG // 2):
            ws = []
            for e in (g * HPG + 2 * p, g * HPG + 2 * p + 1):
                seg = cum[:, e:e + 1] - cum_t[e:e + 1, :]
                dec = jnp.exp(jnp.where(causal, seg, -jnp.inf))
                ws.append(((cb * dec) * dt_t[e:e + 1, :]).astype(BF16))
            w2 = jnp.concatenate(ws, axis=1)
            col = (g * HPG + 2 * p) * HEADDIM
            x2 = xs[:, col:col + LANES]
            r2 = jnp.concatenate([jnp.where(first_head, x2, 0.0),
                                  jnp.where(first_head, 0.0, x2)], axis=0).astype(BF16)
            ys.append(_dot(w2, r2))
    y_diag = jnp.concatenate(ys, axis=1)
    return xs, cm, bm, y_diag, coef_x, ecum_x


def _gated_norm(y, z, nrm_row):
    return _rms(y * jax.nn.silu(z), nrm_row).astype(BF16)


def _ssd_prompt_kernel(xbc_ref, dt_ref, z_ref, e_ref, w8_ref, cb_ref, alog_ref, dx_ref, nrm_ref,
                       yn_ref, hout_ref, xp_ref, ht_ref):
    c = pl.program_id(1)

    @pl.when(c == 0)
    def _():
        xp_ref[0:SUBLANES, :] = jnp.zeros((SUBLANES, CONV_DIM), F32)
        ht_ref[...] = jnp.zeros_like(ht_ref)

    xp_ref[SUBLANES:SUBLANES + CHUNK, :] = xbc_ref[...]
    strips = _conv4_silu(lambda cols: xp_ref[:, cols], w8_ref, cb_ref)
    xp_ref[0:SUBLANES, :] = xbc_ref[CHUNK - SUBLANES:CHUNK, :]

    xs, cm, bm, y_diag, coef_x, ecum_x = _ssd_intra(strips, dt_ref[...], alog_ref[...], e_ref[...], CHUNK)
    xcoef = (xs * coef_x).astype(BF16)
    dec_row = ecum_x[CHUNK - 1:CHUNK, :]
    y_off = []
    for g in range(GROUPS):
        gs = slice(g * GW, (g + 1) * GW)
        ht_g = ht_ref[:, gs]
        cg = cm[:, g * STATE:(g + 1) * STATE].astype(BF16)
        y_off.append(_dot(cg, ht_g.astype(BF16)))
        bg_t = bm[:, g * STATE:(g + 1) * STATE].T.astype(BF16)
        ht_ref[:, gs] = ht_g * dec_row[:, gs] + _dot(bg_t, xcoef[:, gs])
    y = y_diag + jnp.concatenate(y_off, axis=1) * ecum_x + dx_ref[...] * xs
    yn_ref[...] = _gated_norm(y, z_ref[...], nrm_ref[...])

    @pl.when(c == pl.num_programs(1) - 1)
    def _():
        hout_ref[...] = ht_ref[...].T


def _ssm_param_specs(const):
    return [
        pl.BlockSpec((LANES, DINNER), const),
        pl.BlockSpec((SSM_K, SUBLANES, CONV_DIM), lambda *_: (0, 0, 0)),
        pl.BlockSpec((1, CONV_DIM), const),
        pl.BlockSpec((1, LANES), const),
        pl.BlockSpec((1, DINNER), const),
        pl.BlockSpec((1, DINNER), const),
    ]


def _ssd_prompt(xbc, dt, z, *params):
    nc = SEQ // CHUNK
    blk = lambda b, c: (b * nc + c, 0)
    const = lambda b, c: (0, 0)
    return pl.pallas_call(
        _ssd_prompt_kernel,
        grid=(BATCH, nc),
        in_specs=[
            pl.BlockSpec((CHUNK, CONV_DIM), blk),
            pl.BlockSpec((CHUNK, LANES), blk),
            pl.BlockSpec((CHUNK, DINNER), blk),
        ] + _ssm_param_specs(const),
        out_specs=[
            pl.BlockSpec((CHUNK, DINNER), blk),
            pl.BlockSpec((None, DINNER, STATE), lambda b, c: (b, 0, 0)),
        ],
        out_shape=[
            jax.ShapeDtypeStruct((N_TOK, DINNER), BF16),
            jax.ShapeDtypeStruct((BATCH, DINNER, STATE), F32),
        ],
        scratch_shapes=[pltpu.VMEM((CHUNK + SUBLANES, CONV_DIM), F32), pltpu.VMEM((STATE, DINNER), F32)],
        compiler_params=_cparams(("parallel", "arbitrary")),
        name="ssd_prompt",
    )(xbc, dt, z, *params)


def _ssd_sample_kernel(yn_any, xbc_ref, cst_ref, dt_ref, z_ref, h0_ref, e_ref, w8_ref, cb_ref,
                       alog_ref, dx_ref, nrm_ref, yn_ref, hout_ref,
                       xp_ref, cm_ref, bt_ref, xcoef_ref, ecx_ref, y_ref):
    del yn_any
    j = pl.program_id(1)

    @pl.when(j == 0)
    def _():
        xp_ref[:, SUBLANES - (SSM_K - 1):SUBLANES, :] = cst_ref[...]
        xp_ref[:, SUBLANES:SUBLANES + DEC_SEQ, :] = xbc_ref[...]
        strips = _conv4_silu(lambda cols: xp_ref[:, :, cols], w8_ref, cb_ref)
        xs, cm, bm, y_diag, coef_x, ecum_x = _ssd_intra(
            strips, dt_ref[...], alog_ref[...], e_ref[...], DEC_SEQ)
        cm_ref[...] = cm
        for g in range(GROUPS):
            bt_ref[g] = bm[:, g * STATE:(g + 1) * STATE].T
        xcoef_ref[...] = (xs * coef_x).astype(BF16)
        ecx_ref[...] = ecum_x
        y_ref[...] = y_diag + dx_ref[...] * xs

    col = lax.broadcasted_iota(jnp.int32, (STATE, CHUNK), 1)
    base = j * (SEQ_PER_STEP * DEC_SEQ)
    for p in range(SEQ_PER_STEP):
        r0 = pl.multiple_of(base + p * DEC_SEQ, DEC_SEQ)
        rows = pl.ds(r0, DEC_SEQ)
        ht = h0_ref[p].T
        ht_b = ht.astype(BF16)
        ecx = ecx_ref[rows, :]
        cj = cm_ref[rows, :].astype(BF16)
        mine = jnp.logical_and(col >= r0, col < r0 + DEC_SEQ)
        y_off, new = [], []
        for g in range(GROUPS):
            gs = slice(g * GW, (g + 1) * GW)
            y_off.append(_dot(cj[:, g * STATE:(g + 1) * STATE], ht_b[:, gs]))
            bg_t = jnp.where(mine, bt_ref[g], 0.0).astype(BF16)
            new.append(_dot(bg_t, xcoef_ref[:, gs]))
        y_ref[rows, :] += jnp.concatenate(y_off, axis=1) * ecx
        ht_new = ht * ecx[DEC_SEQ - 1:DEC_SEQ, :] + jnp.concatenate(new, axis=1)
        hout_ref[p] = ht_new.T

    @pl.when(j == pl.num_programs(1) - 1)
    def _():
        yn_ref[...] = _gated_norm(y_ref[...], z_ref[...], nrm_ref[...])


def _ssd_sample(yn_all, xbc3, cst, dt, z, h0, *params):
    nb = DEC_BATCH // SEGS
    off = N_PROMPT // CHUNK
    const = lambda i, j: (0, 0)
    tok = lambda i, j: (off + i, 0)
    seq = lambda i, j: (i * (SEGS // SEQ_PER_STEP) + j, 0, 0)
    return pl.pallas_call(
        _ssd_sample_kernel,
        grid=(nb, SEGS // SEQ_PER_STEP),
        in_specs=[
            pl.BlockSpec(memory_space=pl.ANY),
            pl.BlockSpec((SEGS, DEC_SEQ, CONV_DIM), lambda i, j: (off + i, 0, 0)),
            pl.BlockSpec((SEGS, SSM_K - 1, CONV_DIM), lambda i, j: (i, 0, 0)),
            pl.BlockSpec((CHUNK, LANES), tok),
            pl.BlockSpec((CHUNK, DINNER), tok),
            pl.BlockSpec((SEQ_PER_STEP, DINNER, STATE), seq),
        ] + _ssm_param_specs(const),
        out_specs=[
            pl.BlockSpec((CHUNK, DINNER), tok),
            pl.BlockSpec((SEQ_PER_STEP, DINNER, STATE), seq),
        ],
        out_shape=[
            jax.ShapeDtypeStruct((N_TOK, DINNER), BF16),
            jax.ShapeDtypeStruct((DEC_BATCH, DINNER, STATE), F32),
        ],
        scratch_shapes=[
            pltpu.VMEM((SEGS, SUBLANES + DEC_SEQ, CONV_DIM), F32),
            pltpu.VMEM((CHUNK, GROUPS * STATE), F32),
            pltpu.VMEM((GROUPS, STATE, CHUNK), F32),
            pltpu.VMEM((CHUNK, DINNER), BF16),
            pltpu.VMEM((CHUNK, DINNER), F32),
            pltpu.VMEM((CHUNK, DINNER), F32),
        ],
        input_output_aliases={0: 0},
        compiler_params=_cparams(("parallel", "arbitrary")),
        name="ssd_sample",
    )(yn_all, xbc3, cst, dt, z, h0, *params)


CC_PAD = 32
CC_FIRST = CC_PAD - (CC_K - 1)
CC_LN_ROWS = 32


def _cc_kernel(*refs, n_seg, seg_len, carry, aliased):
    refs = list(refs)
    if aliased:
        refs.pop(0)
    u_ref = refs.pop(0)
    st_ref = None if carry else refs.pop(0)
    w8_ref, b_ref, g_ref, be_ref, o_ref, xp_ref, res_ref = refs
    if carry:
        @pl.when(pl.program_id(1) == 0)
        def _():
            xp_ref[:, 0:CC_PAD, :] = jnp.zeros((n_seg, CC_PAD, CC_DIM), F32)
    else:
        xp_ref[:, CC_FIRST:CC_PAD, :] = st_ref[...]
    xp_ref[:, CC_PAD:CC_PAD + seg_len, :] = u_ref[...]

    rb = min(seg_len, 64)
    n_shift = rb + CC_PAD - SUBLANES
    for s in range(n_seg):
        for t0 in range(0, seg_len, rb):
            for c0 in range(0, CC_DIM, LANES):
                cols = slice(c0, c0 + LANES)
                full = xp_ref[s, t0:t0 + rb + CC_PAD, cols]
                acc = None
                for b in range(SUBLANES):
                    xb = full if b == 0 else full[b:b + n_shift, :]
                    for a in range(CC_PAD // SUBLANES + 1):
                        k = SUBLANES * a + b - CC_FIRST
                        if k < 0 or k >= CC_K:
                            continue
                        win = xb[SUBLANES * a:SUBLANES * a + rb, :].reshape(rb // SUBLANES, SUBLANES, LANES)
                        t = w8_ref[k, :, cols][None] * win
                        acc = t if acc is None else acc + t
                row = s * seg_len + t0
                res_ref[row:row + rb, cols] = acc.reshape(rb, LANES) + b_ref[:, cols]

    for r0 in range(0, n_seg * seg_len, CC_LN_ROWS):
        v = res_ref[r0:r0 + CC_LN_ROWS, :]
        mu = jnp.mean(v, axis=-1, keepdims=True)
        d = v - mu
        var = jnp.mean(d * d, axis=-1, keepdims=True)
        yv = (d * lax.rsqrt(var + EPS)) * g_ref[...] + be_ref[...]
        o_ref[r0:r0 + CC_LN_ROWS, :] = jax.nn.silu(yv).astype(BF16)
    if carry:
        xp_ref[:, 0:CC_PAD, :] = xp_ref[:, seg_len:seg_len + CC_PAD, :]


def _cc_param_specs(const):
    return [
        pl.BlockSpec((CC_K, SUBLANES, CC_DIM), lambda *_: (0, 0, 0)),
        pl.BlockSpec((1, CC_DIM), const),
        pl.BlockSpec((1, CC_DIM), const),
        pl.BlockSpec((1, CC_DIM), const),
    ]


def _cc_prompt(u3, *params, tq):
    nc = SEQ // tq
    const = lambda i, c: (0, 0)
    return pl.pallas_call(
        functools.partial(_cc_kernel, n_seg=1, seg_len=tq, carry=True, aliased=False),
        grid=(BATCH, nc),
        in_specs=[pl.BlockSpec((1, tq, CC_DIM), lambda i, c: (i * nc + c, 0, 0))] + _cc_param_specs(const),
        out_specs=pl.BlockSpec((tq, CC_DIM), lambda i, c: (i * nc + c, 0)),
        out_shape=jax.ShapeDtypeStruct((N_TOK, CC_DIM), BF16),
        scratch_shapes=[pltpu.VMEM((1, CC_PAD + tq, CC_DIM), F32), pltpu.VMEM((tq, CC_DIM), F32)],
        compiler_params=_cparams(("parallel", "arbitrary")),
        name="cc_prompt",
    )(u3, *params)


def _cc_sample(uc_all, u3, st, *params, n_seg):
    const = lambda i: (0, 0)
    rows = n_seg * DEC_SEQ
    off = N_PROMPT // rows
    return pl.pallas_call(
        functools.partial(_cc_kernel, n_seg=n_seg, seg_len=DEC_SEQ, carry=False, aliased=True),
        grid=(DEC_BATCH // n_seg,),
        in_specs=[
            pl.BlockSpec(memory_space=pl.ANY),
            pl.BlockSpec((n_seg, DEC_SEQ, CC_DIM), lambda i: (off + i, 0, 0)),
            pl.BlockSpec((n_seg, CC_K - 1, CC_DIM), lambda i: (i, 0, 0)),
        ] + _cc_param_specs(const),
        out_specs=pl.BlockSpec((rows, CC_DIM), lambda i: (off + i, 0)),
        out_shape=jax.ShapeDtypeStruct((N_TOK, CC_DIM), BF16),
        scratch_shapes=[pltpu.VMEM((n_seg, CC_PAD + DEC_SEQ, CC_DIM), F32),
                        pltpu.VMEM((rows, CC_DIM), F32)],
        input_output_aliases={0: 0},
        compiler_params=_cparams(("parallel",)),
        name="cc_sample",
    )(uc_all, u3, st, *params)


def _merge_kernel(h_ref, yn_ref, uc_ref, wgs_ref, wgc_ref, wso_ref, wco_ref, o_ref):
    h = h_ref[...]
    g_ssd = jax.nn.sigmoid(_dot(h, wgs_ref[...]))
    g_cc = jax.nn.sigmoid(_dot(h, wgc_ref[...]))
    ssd_out = _dot(yn_ref[...], wso_ref[...])
    cc_out = _dot(uc_ref[...], wco_ref[...])
    o_ref[...] = (g_ssd * ssd_out + g_cc * cc_out).astype(BF16)


def _merge(h, yn, uc, wgs, wgc, wso, wco, *, tm=1024, tn=512):
    m = h.shape[0]
    row = lambda i, j: (i, 0)
    colw = lambda i, j: (0, j)
    return pl.pallas_call(
        _merge_kernel,
        grid=(m // tm, D_MODEL // tn),
        in_specs=[
            pl.BlockSpec((tm, D_MODEL), row),
            pl.BlockSpec((tm, DINNER), row),
            pl.BlockSpec((tm, CC_DIM), row),
            pl.BlockSpec((D_MODEL, tn), colw),
            pl.BlockSpec((D_MODEL, tn), colw),
            pl.BlockSpec((DINNER, tn), colw),
            pl.BlockSpec((CC_DIM, tn), colw),
        ],
        out_specs=pl.BlockSpec((tm, tn), lambda i, j: (i, j)),
        out_shape=jax.ShapeDtypeStruct((m, D_MODEL), BF16),
        compiler_params=_cparams(("parallel", "arbitrary")),
        name="merge",
    )(h, yn, uc, wgs, wgc, wso, wco)


def _oproj_kernel(m_ref, x_ref, w_ref, nrm_ref, o_ref):
    o_ref[...] = x_ref[...] + _rms(_dot(m_ref[...], w_ref[...]), nrm_ref[...])


def _oproj(mg, x, w, nrm, *, tm=512):
    m = x.shape[0]
    row = lambda i: (i, 0)
    const = lambda i: (0, 0)
    return pl.pallas_call(
        _oproj_kernel,
        grid=(m // tm,),
        in_specs=[
            pl.BlockSpec((tm, D_MODEL), row),
            pl.BlockSpec((tm, D_MODEL), row),
            pl.BlockSpec((D_MODEL, D_MODEL), const),
            pl.BlockSpec((1, D_MODEL), const),
        ],
        out_specs=pl.BlockSpec((tm, D_MODEL), row),
        out_shape=jax.ShapeDtypeStruct((m, D_MODEL), F32),
        compiler_params=_cparams(("parallel",)),
        name="oproj",
    )(mg, x, w, nrm)


def _row(v):
    return v.reshape(1, -1).astype(F32)


def _pad_lanes(v):
    return jnp.pad(v.astype(F32), (0, LANES - v.shape[0])).reshape(1, LANES)


def _taps_on_sublanes(w):
    return jnp.broadcast_to(w.astype(F32)[:, None, :], (w.shape[0], SUBLANES, w.shape[1]))


def kernel(x_prompt, x_sample, state_ssm, state_ssm_conv, state_cc_conv, ffn1_pre_norm, ffn1_post_norm, ffn1_w_gate, ffn1_w_up, ffn1_w_down, mix_pre_norm, mix_post_norm, w_in, ssm_conv_w, ssm_conv_b, ssm_dt_bias, ssm_A_log, ssm_D, ssm_norm, w_ssd_out, cc_conv_w, cc_conv_b, cc_ln_g, cc_ln_b, w_cc_out, w_o, ffn2_pre_norm, ffn2_post_norm, ffn2_w_gate, ffn2_w_up, ffn2_w_down):
    l = 0
    bf = lambda w: w.astype(BF16)
    npt, nst = N_PROMPT // FFN_TM, N_SAMPLE // FFN_TM

    f1 = (_row(ffn1_pre_norm[l]), _row(ffn1_post_norm[l]), _row(mix_pre_norm[l]),
          bf(ffn1_w_gate[l]), bf(ffn1_w_up[l]), bf(ffn1_w_down[l]))
    later = (ffn2_w_gate[l], ffn2_w_up[l], ffn2_w_down[l], w_in[l], w_ssd_out[l], w_cc_out[l], w_o[l])
    x1, h, w2g, w2u, w2d, wb, w_so, w_co, w_ob = _ffn(
        x_prompt.reshape(N_PROMPT, D_MODEL), 0, npt, *f1, out_rows=N_TOK, out_tile0=0, emit_h=True,
        casts=later, name="ffn1_prompt")
    x1, h = _ffn(x_sample.reshape(N_SAMPLE, D_MODEL), 0, nst, *f1, out_rows=N_TOK, out_tile0=npt,
                 emit_h=True, prev=(x1, h), name="ffn1_sample")

    c_xbc = DINNER
    c_dt = c_xbc + CONV_DIM
    c2 = c_dt + HEADS
    c3 = c2 + CC_DIM
    c4 = c3 + CC_DIM
    c5 = c4 + D_MODEL
    w_ga, w_gb = wb[:, c2:c3], wb[:, c3:c4]
    w_gs, w_gc = wb[:, c4:c5], wb[:, c5:]

    e01 = (jnp.arange(LANES)[:, None] == (jnp.arange(DINNER)[None, :] // HEADDIM)).astype(BF16)
    d_x = jnp.repeat(ssm_D[l].astype(F32), HEADDIM).reshape(1, DINNER)

    z = _mm(_mm_plain_kernel, h, [(wb, 0)], [], DINNER, F32, "proj_z", tn=WIDE_TN)
    xbc = _mm(_mm_plain_kernel, h, [(wb, c_xbc)], [], CONV_DIM, F32, "proj_xbc", tn=WIDE_TN)
    dt = _mm(_mm_dt_kernel, h, [(wb, c_dt)], [_pad_lanes(ssm_dt_bias[l])], LANES, F32, "proj_dt")
    u = _mm(_mm_glu_kernel, h, [(w_ga, 0), (w_gb, 0)], [], CC_DIM, F32, "proj_glu")

    ssm_params = (e01, _taps_on_sublanes(ssm_conv_w[l]), _row(ssm_conv_b[l]), _pad_lanes(ssm_A_log[l]),
                  d_x, _row(ssm_norm[l]))
    yn, ssm_p = _ssd_prompt(xbc, dt, z, *ssm_params)
    yn, ssm_s = _ssd_sample(yn, xbc.reshape(N_TOK // DEC_SEQ, DEC_SEQ, CONV_DIM), state_ssm_conv[l], dt, z,
                            state_ssm[l].reshape(DEC_BATCH, DINNER, STATE), *ssm_params)

    cc_params = (_taps_on_sublanes(cc_conv_w[l]), _row(cc_conv_b[l]), _row(cc_ln_g[l]), _row(cc_ln_b[l]))
    tq = 256
    uc = _cc_prompt(u.reshape(N_TOK // tq, tq, CC_DIM), *cc_params, tq=tq)
    uc = _cc_sample(uc, u.reshape(N_TOK // DEC_SEQ, DEC_SEQ, CC_DIM), state_cc_conv[l], *cc_params, n_seg=16)

    mg = _merge(h, yn, uc, w_gs, w_gc, w_so, w_co)
    x2 = _oproj(mg, x1, w_ob, _row(mix_post_norm[l]))

    f2 = (_row(ffn2_pre_norm[l]), _row(ffn2_post_norm[l]), _row(ffn2_post_norm[l]), w2g, w2u, w2d)
    (y_p,) = _ffn(x2, 0, npt, *f2, out_rows=N_PROMPT, out_tile0=0, emit_h=False, name="ffn2_prompt")
    (y_s,) = _ffn(x2, npt, nst, *f2, out_rows=N_SAMPLE, out_tile0=0, emit_h=False, name="ffn2_sample")

    y_prompt = y_p.reshape(BATCH, SEQ, D_MODEL)
    y_sample = y_s.reshape(DEC_BATCH, DEC_SEQ, D_MODEL)
    new_ssm_p = ssm_p.reshape(1, BATCH, HEADS, HEADDIM, STATE)
    new_ssm_s = ssm_s.reshape(1, DEC_BATCH, HEADS, HEADDIM, STATE)
    tail = lambda a, n: jnp.stack([a[(b + 1) * SEQ - n:(b + 1) * SEQ] for b in range(BATCH)])[None]
    new_sconv_p = tail(xbc, SSM_K - 1)
    new_cc_p = tail(u, CC_K - 1)
    xbc_s3 = xbc[N_PROMPT:].reshape(DEC_BATCH, DEC_SEQ, CONV_DIM)
    new_sconv_s = xbc_s3[:, DEC_SEQ - (SSM_K - 1):][None]
    u_s3 = u[N_PROMPT:].reshape(DEC_BATCH, DEC_SEQ, CC_DIM)
    new_cc_s = jnp.concatenate([state_cc_conv[l][:, DEC_SEQ:], u_s3], axis=1)[None]
    return (y_prompt, y_sample, new_ssm_p, new_sconv_p, new_cc_p, new_ssm_s, new_sconv_s, new_cc_s)
```

```python
import functools

import jax
import jax.numpy as jnp
from jax import lax
from jax.experimental import pallas as pl
from jax.experimental.pallas import tpu as pltpu

F32 = jnp.float32
BF16 = jnp.bfloat16

D_MODEL = 2048
D_FF = 5632
BATCH, SEQ = 4, 2048
DEC_BATCH, DEC_SEQ = 128, 8
N_PROMPT = BATCH * SEQ
N_SAMPLE = DEC_BATCH * DEC_SEQ
N_TOK = N_PROMPT + N_SAMPLE
HEADS, HEADDIM, GROUPS, STATE = 32, 64, 4, 128
HPG = HEADS // GROUPS
DINNER = HEADS * HEADDIM
GW = HPG * HEADDIM
SSM_K = 4
CONV_DIM = DINNER + 2 * GROUPS * STATE
CC_DIM = D_MODEL // 2
CC_K = 31
EPS = 1e-6

LANES = 128
SUBLANES = 8
CHUNK = 128
SEGS = CHUNK // DEC_SEQ
SEQ_PER_STEP = 4
VMEM_LIMIT = 48 * 1024 * 1024
FFN_VMEM_LIMIT = 58 * 1024 * 1024


def _cparams(sem, vmem=VMEM_LIMIT):
    return pltpu.CompilerParams(dimension_semantics=sem, vmem_limit_bytes=vmem)


def _rms(x, g):
    ms = jnp.mean(x * x, axis=-1, keepdims=True)
    return (x * lax.rsqrt(ms + EPS)) * g


def _dot(a, b):
    return jnp.dot(a, b, preferred_element_type=F32)


def _split3(x):
    hi = x.astype(BF16)
    r = x - hi.astype(F32)
    mid = r.astype(BF16)
    lo = (r - mid.astype(F32)).astype(BF16)
    return hi, mid, lo


def _dot01_rhs(x, m01):
    hi, mid, lo = _split3(x)
    return _dot(hi, m01) + _dot(mid, m01) + _dot(lo, m01)


def _dot01_lhs(m01, x):
    hi, mid, lo = _split3(x)
    return _dot(m01, hi) + _dot(m01, mid) + _dot(m01, lo)


BF16_ROWS = 16
FFN_TM = 512
FFN_TF = 512


def _ffn_kernel(*refs, n_prev, n_cast, emit_h):
    refs = refs[n_prev:]
    x_ref, pre_ref, post_ref, nxt_ref, wg_ref, wu_ref, wd_ref = refs[:7]
    src_refs = refs[7:7 + n_cast]
    outs = refs[7 + n_cast:]
    y_ref = outs[0]
    ho_ref = outs[1] if emit_h else None
    dst_refs = outs[1 + emit_h:1 + emit_h + n_cast]
    h_scr, acc_scr = outs[1 + emit_h + n_cast:]
    j = pl.program_id(1)

    @pl.when(j == 0)
    def _():
        h_scr[...] = _rms(x_ref[...], pre_ref[...]).astype(BF16)
        acc_scr[...] = jnp.zeros_like(acc_scr)

    for src_ref, dst_ref in zip(src_refs, dst_refs):
        dst_ref[...] = src_ref[...].astype(BF16)

    h = h_scr[...]
    g = _dot(h, wg_ref[...])
    u = _dot(h, wu_ref[...])
    a = (jax.nn.silu(g) * u).astype(BF16)
    acc_scr[...] += _dot(a, wd_ref[...])

    @pl.when(j == pl.num_programs(1) - 1)
    def _():
        xn = x_ref[...] + 0.5 * _rms(acc_scr[...], post_ref[...])
        y_ref[...] = xn
        if emit_h:
            ho_ref[...] = _rms(xn, nxt_ref[...]).astype(BF16)


def _cast_chunk_rows(rows, n_steps):
    return BF16_ROWS * pl.cdiv(pl.cdiv(rows, BF16_ROWS), n_steps)


def _ffn(x, x_tile0, n_tiles, pre, post, nxt, wg, wu, wd, *, out_rows, out_tile0, emit_h,
         prev=(), casts=(), name, tm=FFN_TM, tf=FFN_TF):
    nj = D_FF // tf
    n_steps = n_tiles * nj
    vec = pl.BlockSpec((1, D_MODEL), lambda i, j: (0, 0))
    tile = lambda t0: pl.BlockSpec((tm, D_MODEL), functools.partial(lambda i, j, t: (t + i, 0), t=t0))
    out_shape = [jax.ShapeDtypeStruct((out_rows, D_MODEL), F32)]
    out_specs = [tile(out_tile0)]
    if emit_h:
        out_shape.append(jax.ShapeDtypeStruct((out_rows, D_MODEL), BF16))
        out_specs.append(tile(out_tile0))
    cast_specs = []
    for w in casts:
        rows, cols = w.shape
        rc = _cast_chunk_rows(rows, n_steps)
        spec = pl.BlockSpec((rc, cols), functools.partial(
            lambda i, j, last: (jnp.minimum(i * nj + j, last), 0), last=pl.cdiv(rows, rc) - 1))
        cast_specs.append(spec)
        out_shape.append(jax.ShapeDtypeStruct((rows, cols), BF16))
        out_specs.append(spec)
    return pl.pallas_call(
        functools.partial(_ffn_kernel, n_prev=len(prev), n_cast=len(casts), emit_h=emit_h),
        grid=(n_tiles, nj),
        in_specs=[pl.BlockSpec(memory_space=pl.ANY)] * len(prev) + [
            tile(x_tile0), vec, vec, vec,
            pl.BlockSpec((D_MODEL, tf), lambda i, j: (0, j)),
            pl.BlockSpec((D_MODEL, tf), lambda i, j: (0, j)),
            pl.BlockSpec((tf, D_MODEL), lambda i, j: (j, 0)),
        ] + cast_specs,
        out_specs=out_specs,
        out_shape=out_shape,
        scratch_shapes=[pltpu.VMEM((tm, D_MODEL), BF16), pltpu.VMEM((tm, D_MODEL), F32)],
        input_output_aliases={k: k for k in range(len(prev))},
        compiler_params=_cparams(("arbitrary", "arbitrary"), FFN_VMEM_LIMIT),
        name=name,
    )(*prev, x, pre, post, nxt, wg, wu, wd, *casts)


WIDE_TN = 1024


def _dot_nt(a, b_t):
    return lax.dot_general(a, b_t, (((1,), (1,)), ((), ())), preferred_element_type=F32)


def _mm_plain_kernel(h_ref, w_ref, o_ref):
    o_ref[...] = _dot_nt(h_ref[...], w_ref[...]).astype(o_ref.dtype)


def _mm_dt_kernel(h_ref, w_ref, b_ref, o_ref):
    dt = jax.nn.softplus(_dot_nt(h_ref[...], w_ref[...]) + b_ref[...])
    lane = lax.broadcasted_iota(jnp.int32, dt.shape, 1)
    o_ref[...] = jnp.where(lane < HEADS, dt, 0.0)


def _mm_glu_kernel(h_ref, wa_ref, wb_ref, o_ref):
    h = h_ref[...]
    o_ref[...] = _dot_nt(h, wa_ref[...]) * jax.nn.sigmoid(_dot_nt(h, wb_ref[...]))


def _mm(kernel, h, ws, extras, n_out, out_dtype, name, *, tm=1024, tn=512):
    m, k = h.shape
    tn = min(tn, n_out)
    grid = (m // tm, n_out // tn)
    in_specs = [pl.BlockSpec((tm, k), lambda i, j: (i, 0))]
    for _, r0 in ws:
        assert r0 % tn == 0
        in_specs.append(pl.BlockSpec((tn, k), functools.partial(lambda i, j, o: (o + j, 0), o=r0 // tn)))
    in_specs += [pl.BlockSpec((1, tn), lambda i, j: (0, j)) for _ in extras]
    return pl.pallas_call(
        kernel,
        grid=grid,
        in_specs=in_specs,
        out_specs=pl.BlockSpec((tm, tn), lambda i, j: (i, j)),
        out_shape=jax.ShapeDtypeStruct((m, n_out), out_dtype),
        compiler_params=_cparams(("parallel", "arbitrary")),
        name=name,
    )(h, *[w for w, _ in ws], *extras)


def _block_masks(seg_len):
    t = lax.broadcasted_iota(jnp.int32, (CHUNK, CHUNK), 0)
    s = lax.broadcasted_iota(jnp.int32, (CHUNK, CHUNK), 1)
    if seg_len == CHUNK:
        same = t >= 0
    else:
        sh = seg_len.bit_length() - 1
        same = (t >> sh) == (s >> sh)
    causal = jnp.logical_and(same, s <= t)
    return same, causal


CONV_STRIP = 512


def _conv4_silu(load_strip, w8_ref, cb_ref):
    strips = []
    for c0 in range(0, CONV_DIM, CONV_STRIP):
        cols = slice(c0, c0 + CONV_STRIP)
        full = load_strip(cols)
        n = full.shape[-2] - SUBLANES
        acc = None
        for k in range(SSM_K):
            win = full[..., 5 + k:5 + k + n, :].reshape(CHUNK // SUBLANES, SUBLANES, CONV_STRIP)
            t = w8_ref[k, :, cols][None] * win
            acc = t if acc is None else acc + t
        strips.append(jax.nn.silu(acc.reshape(CHUNK, CONV_STRIP) + cb_ref[:, cols]))
    return strips


def _ssd_intra(strips, dt, alog_row, e01, seg_len):
    same, causal = _block_masks(seg_len)
    n_x = DINNER // CONV_STRIP
    xs = jnp.concatenate(strips[:n_x], axis=1)
    bm = strips[n_x]
    cm = strips[n_x + 1]
    a_row = -jnp.exp(alog_row)
    dta = dt * a_row
    causal01 = jnp.where(causal, 1.0, 0.0).astype(BF16)
    same01 = jnp.where(same, 1.0, 0.0).astype(BF16)
    cum = _dot01_lhs(causal01, dta)
    cum_end = _dot01_lhs(same01, dta)
    cum_t = cum.T
    dt_t = dt.T
    ecum_x = _dot01_rhs(jnp.exp(cum), e01)
    coef_x = _dot01_rhs(jnp.exp(cum_end - cum) * dt, e01)

    lane = lax.broadcasted_iota(jnp.int32, (CHUNK, LANES), 1)
    first_head = lane < HEADDIM
    ys = []
    for g in range(GROUPS):
        cg = cm[:, g * STATE:(g + 1) * STATE].astype(BF16)
        bg = bm[:, g * STATE:(g + 1) * STATE].astype(BF16)
        cb = lax.dot_general(cg, bg, (((1,), (1,)), ((), ())), preferred_element_type=F32)
        for p in range(HPG // 2):
            ws = []
            for e in (g * HPG + 2 * p, g * HPG + 2 * p + 1):
                seg = cum[:, e:e + 1] - cum_t[e:e + 1, :]
                dec = jnp.exp(jnp.where(causal, seg, -jnp.inf))
                ws.append(((cb * dec) * dt_t[e:e + 1, :]).astype(BF16))
            w2 = jnp.concatenate(ws, axis=1)
            col = (g * HPG + 2 * p) * HEADDIM
            x2 = xs[:, col:col + LANES]
            r2 = jnp.concatenate([jnp.where(first_head, x2, 0.0),
                                  jnp.where(first_head, 0.0, x2)], axis=0).astype(BF16)
            ys.append(_dot(w2, r2))
    y_diag = jnp.concatenate(ys, axis=1)
    return xs, cm, bm, y_diag, coef_x, ecum_x


def _gated_norm(y, z, nrm_row):
    return _rms(y * jax.nn.silu(z), nrm_row).astype(BF16)


def _ssd_prompt_kernel(xbc_ref, dt_ref, z_ref, e_ref, w8_ref, cb_ref, alog_ref, dx_ref, nrm_ref,
                       yn_ref, hout_ref, xp_ref, ht_ref):
    c = pl.program_id(1)

    @pl.when(c == 0)
    def _():
        xp_ref[0:SUBLANES, :] = jnp.zeros((SUBLANES, CONV_DIM), F32)
        ht_ref[...] = jnp.zeros_like(ht_ref)

    xp_ref[SUBLANES:SUBLANES + CHUNK, :] = xbc_ref[...]
    strips = _conv4_silu(lambda cols: xp_ref[:, cols], w8_ref, cb_ref)
    xp_ref[0:SUBLANES, :] = xbc_ref[CHUNK - SUBLANES:CHUNK, :]

    xs, cm, bm, y_diag, coef_x, ecum_x = _ssd_intra(strips, dt_ref[...], alog_ref[...], e_ref[...], CHUNK)
    xcoef = (xs * coef_x).astype(BF16)
    dec_row = ecum_x[CHUNK - 1:CHUNK, :]
    y_off = []
    for g in range(GROUPS):
        gs = slice(g * GW, (g + 1) * GW)
        ht_g = ht_ref[:, gs]
        cg = cm[:, g * STATE:(g + 1) * STATE].astype(BF16)
        y_off.append(_dot(cg, ht_g.astype(BF16)))
        bg_t = bm[:, g * STATE:(g + 1) * STATE].T.astype(BF16)
        ht_ref[:, gs] = ht_g * dec_row[:, gs] + _dot(bg_t, xcoef[:, gs])
    y = y_diag + jnp.concatenate(y_off, axis=1) * ecum_x + dx_ref[...] * xs
    yn_ref[...] = _gated_norm(y, z_ref[...], nrm_ref[...])

    @pl.when(c == pl.num_programs(1) - 1)
    def _():
        hout_ref[...] = ht_ref[...].T


def _ssm_param_specs(const):
    return [
        pl.BlockSpec((LANES, DINNER), const),
        pl.BlockSpec((SSM_K, SUBLANES, CONV_DIM), lambda *_: (0, 0, 0)),
        pl.BlockSpec((1, CONV_DIM), const),
        pl.BlockSpec((1, LANES), const),
        pl.BlockSpec((1, DINNER), const),
        pl.BlockSpec((1, DINNER), const),
    ]


def _ssd_prompt(xbc, dt, z, *params):
    nc = SEQ // CHUNK
    blk = lambda b, c: (b * nc + c, 0)
    const = lambda b, c: (0, 0)
    return pl.pallas_call(
        _ssd_prompt_kernel,
        grid=(BATCH, nc),
        in_specs=[
            pl.BlockSpec((CHUNK, CONV_DIM), blk),
            pl.BlockSpec((CHUNK, LANES), blk),
            pl.BlockSpec((CHUNK, DINNER), blk),
        ] + _ssm_param_specs(const),
        out_specs=[
            pl.BlockSpec((CHUNK, DINNER), blk),
            pl.BlockSpec((None, DINNER, STATE), lambda b, c: (b, 0, 0)),
        ],
        out_shape=[
            jax.ShapeDtypeStruct((N_TOK, DINNER), BF16),
            jax.ShapeDtypeStruct((BATCH, DINNER, STATE), F32),
        ],
        scratch_shapes=[pltpu.VMEM((CHUNK + SUBLANES, CONV_DIM), F32), pltpu.VMEM((STATE, DINNER), F32)],
        compiler_params=_cparams(("parallel", "arbitrary")),
        name="ssd_prompt",
    )(xbc, dt, z, *params)


def _ssd_sample_kernel(yn_any, xbc_ref, cst_ref, dt_ref, z_ref, h0_ref, e_ref, w8_ref, cb_ref,
                       alog_ref, dx_ref, nrm_ref, yn_ref, hout_ref,
                       xp_ref, cm_ref, bt_ref, xcoef_ref, ecx_ref, y_ref):
    del yn_any
    j = pl.program_id(1)

    @pl.when(j == 0)
    def _():
        xp_ref[:, SUBLANES - (SSM_K - 1):SUBLANES, :] = cst_ref[...]
        xp_ref[:, SUBLANES:SUBLANES + DEC_SEQ, :] = xbc_ref[...]
        strips = _conv4_silu(lambda cols: xp_ref[:, :, cols], w8_ref, cb_ref)
        xs, cm, bm, y_diag, coef_x, ecum_x = _ssd_intra(
            strips, dt_ref[...], alog_ref[...], e_ref[...], DEC_SEQ)
        cm_ref[...] = cm
        for g in range(GROUPS):
            bt_ref[g] = bm[:, g * STATE:(g + 1) * STATE].T
        xcoef_ref[...] = (xs * coef_x).astype(BF16)
        ecx_ref[...] = ecum_x
        y_ref[...] = y_diag + dx_ref[...] * xs

    col = lax.broadcasted_iota(jnp.int32, (STATE, CHUNK), 1)
    base = j * (SEQ_PER_STEP * DEC_SEQ)
    for p in range(SEQ_PER_STEP):
        r0 = pl.multiple_of(base + p * DEC_SEQ, DEC_SEQ)
        rows = pl.ds(r0, DEC_SEQ)
        ht = h0_ref[p].T
        ht_b = ht.astype(BF16)
        ecx = ecx_ref[rows, :]
        cj = cm_ref[rows, :].astype(BF16)
        mine = jnp.logical_and(col >= r0, col < r0 + DEC_SEQ)
        y_off, new = [], []
        for g in range(GROUPS):
            gs = slice(g * GW, (g + 1) * GW)
            y_off.append(_dot(cj[:, g * STATE:(g + 1) * STATE], ht_b[:, gs]))
            bg_t = jnp.where(mine, bt_ref[g], 0.0).astype(BF16)
            new.append(_dot(bg_t, xcoef_ref[:, gs]))
        y_ref[rows, :] += jnp.concatenate(y_off, axis=1) * ecx
        ht_new = ht * ecx[DEC_SEQ - 1:DEC_SEQ, :] + jnp.concatenate(new, axis=1)
        hout_ref[p] = ht_new.T

    @pl.when(j == pl.num_programs(1) - 1)
    def _():
        yn_ref[...] = _gated_norm(y_ref[...], z_ref[...], nrm_ref[...])


def _ssd_sample(yn_all, xbc3, cst, dt, z, h0, *params):
    nb = DEC_BATCH // SEGS
    off = N_PROMPT // CHUNK
    const = lambda i, j: (0, 0)
    tok = lambda i, j: (off + i, 0)
    seq = lambda i, j: (i * (SEGS // SEQ_PER_STEP) + j, 0, 0)
    return pl.pallas_call(
        _ssd_sample_kernel,
        grid=(nb, SEGS // SEQ_PER_STEP),
        in_specs=[
            pl.BlockSpec(memory_space=pl.ANY),
            pl.BlockSpec((SEGS, DEC_SEQ, CONV_DIM), lambda i, j: (off + i, 0, 0)),
            pl.BlockSpec((SEGS, SSM_K - 1, CONV_DIM), lambda i, j: (i, 0, 0)),
            pl.BlockSpec((CHUNK, LANES), tok),
            pl.BlockSpec((CHUNK, DINNER), tok),
            pl.BlockSpec((SEQ_PER_STEP, DINNER, STATE), seq),
        ] + _ssm_param_specs(const),
        out_specs=[
            pl.BlockSpec((CHUNK, DINNER), tok),
            pl.BlockSpec((SEQ_PER_STEP, DINNER, STATE), seq),
        ],
        out_shape=[
            jax.ShapeDtypeStruct((N_TOK, DINNER), BF16),
            jax.ShapeDtypeStruct((DEC_BATCH, DINNER, STATE), F32),
        ],
        scratch_shapes=[
            pltpu.VMEM((SEGS, SUBLANES + DEC_SEQ, CONV_DIM), F32),
            pltpu.VMEM((CHUNK, GROUPS * STATE), F32),
            pltpu.VMEM((GROUPS, STATE, CHUNK), F32),
            pltpu.VMEM((CHUNK, DINNER), BF16),
            pltpu.VMEM((CHUNK, DINNER), F32),
            pltpu.VMEM((CHUNK, DINNER), F32),
        ],
        input_output_aliases={0: 0},
        compiler_params=_cparams(("parallel", "arbitrary")),
        name="ssd_sample",
    )(yn_all, xbc3, cst, dt, z, h0, *params)


CC_PAD = 32
CC_FIRST = CC_PAD - (CC_K - 1)
CC_LN_ROWS = 32
CC_STRIP = 512


def _cc_kernel(*refs, n_seg, seg_len, carry, aliased):
    refs = list(refs)
    if aliased:
        refs.pop(0)
    u_ref = refs.pop(0)
    st_ref = None if carry else refs.pop(0)
    w8_ref, b_ref, g_ref, be_ref, o_ref, xp_ref, res_ref = refs
    if carry:
        @pl.when(pl.program_id(1) == 0)
        def _():
            xp_ref[:, 0:CC_PAD, :] = jnp.zeros((n_seg, CC_PAD, CC_DIM), F32)
    else:
        xp_ref[:, CC_FIRST:CC_PAD, :] = st_ref[...]
    xp_ref[:, CC_PAD:CC_PAD + seg_len, :] = u_ref[...]

    rb = min(seg_len, 64)
    n_shift = rb + CC_PAD - SUBLANES
    for s in range(n_seg):
        for t0 in range(0, seg_len, rb):
            for c0 in range(0, CC_DIM, CC_STRIP):
                cols = slice(c0, c0 + CC_STRIP)
                full = xp_ref[s, t0:t0 + rb + CC_PAD, cols]
                acc = None
                for b in range(SUBLANES):
                    xb = full if b == 0 else full[b:b + n_shift, :]
                    for a in range(CC_PAD // SUBLANES + 1):
                        k = SUBLANES * a + b - CC_FIRST
                        if k < 0 or k >= CC_K:
                            continue
                        win = xb[SUBLANES * a:SUBLANES * a + rb, :].reshape(rb // SUBLANES, SUBLANES, CC_STRIP)
                        t = w8_ref[k, :, cols][None] * win
                        acc = t if acc is None else acc + t
                row = s * seg_len + t0
                res_ref[row:row + rb, cols] = acc.reshape(rb, CC_STRIP) + b_ref[:, cols]

    for r0 in range(0, n_seg * seg_len, CC_LN_ROWS):
        v = res_ref[r0:r0 + CC_LN_ROWS, :]
        mu = jnp.mean(v, axis=-1, keepdims=True)
        d = v - mu
        var = jnp.mean(d * d, axis=-1, keepdims=True)
        yv = (d * lax.rsqrt(var + EPS)) * g_ref[...] + be_ref[...]
        o_ref[r0:r0 + CC_LN_ROWS, :] = jax.nn.silu(yv).astype(BF16)
    if carry:
        xp_ref[:, 0:CC_PAD, :] = xp_ref[:, seg_len:seg_len + CC_PAD, :]


def _cc_param_specs(const):
    return [
        pl.BlockSpec((CC_K, SUBLANES, CC_DIM), lambda *_: (0, 0, 0)),
        pl.BlockSpec((1, CC_DIM), const),
        pl.BlockSpec((1, CC_DIM), const),
        pl.BlockSpec((1, CC_DIM), const),
    ]


def _cc_prompt(u3, *params, tq):
    nc = SEQ // tq
    const = lambda i, c: (0, 0)
    return pl.pallas_call(
        functools.partial(_cc_kernel, n_seg=1, seg_len=tq, carry=True, aliased=False),
        grid=(BATCH, nc),
        in_specs=[pl.BlockSpec((1, tq, CC_DIM), lambda i, c: (i * nc + c, 0, 0))] + _cc_param_specs(const),
        out_specs=pl.BlockSpec((tq, CC_DIM), lambda i, c: (i * nc + c, 0)),
        out_shape=jax.ShapeDtypeStruct((N_TOK, CC_DIM), BF16),
        scratch_shapes=[pltpu.VMEM((1, CC_PAD + tq, CC_DIM), F32), pltpu.VMEM((tq, CC_DIM), F32)],
        compiler_params=_cparams(("parallel", "arbitrary")),
        name="cc_prompt",
    )(u3, *params)


def _cc_sample(uc_all, u3, st, *params, n_seg):
    const = lambda i: (0, 0)
    rows = n_seg * DEC_SEQ
    off = N_PROMPT // rows
    return pl.pallas_call(
        functools.partial(_cc_kernel, n_seg=n_seg, seg_len=DEC_SEQ, carry=False, aliased=True),
        grid=(DEC_BATCH // n_seg,),
        in_specs=[
            pl.BlockSpec(memory_space=pl.ANY),
            pl.BlockSpec((n_seg, DEC_SEQ, CC_DIM), lambda i: (off + i, 0, 0)),
            pl.BlockSpec((n_seg, CC_K - 1, CC_DIM), lambda i: (i, 0, 0)),
        ] + _cc_param_specs(const),
        out_specs=pl.BlockSpec((rows, CC_DIM), lambda i: (off + i, 0)),
        out_shape=jax.ShapeDtypeStruct((N_TOK, CC_DIM), BF16),
        scratch_shapes=[pltpu.VMEM((n_seg, CC_PAD + DEC_SEQ, CC_DIM), F32),
                        pltpu.VMEM((rows, CC_DIM), F32)],
        input_output_aliases={0: 0},
        compiler_params=_cparams(("parallel",)),
        name="cc_sample",
    )(uc_all, u3, st, *params)


def _merge_kernel(h_ref, yn_ref, uc_ref, wgs_ref, wgc_ref, wso_ref, wco_ref, o_ref):
    h = h_ref[...]
    g_ssd = jax.nn.sigmoid(_dot_nt(h, wgs_ref[...]))
    g_cc = jax.nn.sigmoid(_dot_nt(h, wgc_ref[...]))
    ssd_out = _dot(yn_ref[...], wso_ref[...])
    cc_out = _dot(uc_ref[...], wco_ref[...])
    o_ref[...] = (g_ssd * ssd_out + g_cc * cc_out).astype(BF16)


def _merge(h, yn, uc, wgs_t, wgc_t, wso, wco, *, tm=1024, tn=512):
    m = h.shape[0]
    row = lambda i, j: (i, 0)
    colw = lambda i, j: (0, j)
    roww = lambda i, j: (j, 0)
    return pl.pallas_call(
        _merge_kernel,
        grid=(m // tm, D_MODEL // tn),
        in_specs=[
            pl.BlockSpec((tm, D_MODEL), row),
            pl.BlockSpec((tm, DINNER), row),
            pl.BlockSpec((tm, CC_DIM), row),
            pl.BlockSpec((tn, D_MODEL), roww),
            pl.BlockSpec((tn, D_MODEL), roww),
            pl.BlockSpec((DINNER, tn), colw),
            pl.BlockSpec((CC_DIM, tn), colw),
        ],
        out_specs=pl.BlockSpec((tm, tn), lambda i, j: (i, j)),
        out_shape=jax.ShapeDtypeStruct((m, D_MODEL), BF16),
        compiler_params=_cparams(("parallel", "arbitrary")),
        name="merge",
    )(h, yn, uc, wgs_t, wgc_t, wso, wco)


def _oproj_kernel(m_ref, x_ref, w_ref, nrm_ref, o_ref):
    o_ref[...] = x_ref[...] + _rms(_dot(m_ref[...], w_ref[...]), nrm_ref[...])


def _oproj(mg, x, w, nrm, *, tm=512):
    m = x.shape[0]
    row = lambda i: (i, 0)
    const = lambda i: (0, 0)
    return pl.pallas_call(
        _oproj_kernel,
        grid=(m // tm,),
        in_specs=[
            pl.BlockSpec((tm, D_MODEL), row),
            pl.BlockSpec((tm, D_MODEL), row),
            pl.BlockSpec((D_MODEL, D_MODEL), const),
            pl.BlockSpec((1, D_MODEL), const),
        ],
        out_specs=pl.BlockSpec((tm, D_MODEL), row),
        out_shape=jax.ShapeDtypeStruct((m, D_MODEL), F32),
        compiler_params=_cparams(("parallel",)),
        name="oproj",
    )(mg, x, w, nrm)


def _row(v):
    return v.reshape(1, -1).astype(F32)


def _pad_lanes(v):
    return jnp.pad(v.astype(F32), (0, LANES - v.shape[0])).reshape(1, LANES)


def _taps_on_sublanes(w):
    return jnp.broadcast_to(w.astype(F32)[:, None, :], (w.shape[0], SUBLANES, w.shape[1]))


def kernel(x_prompt, x_sample, state_ssm, state_ssm_conv, state_cc_conv, ffn1_pre_norm, ffn1_post_norm, ffn1_w_gate, ffn1_w_up, ffn1_w_down, mix_pre_norm, mix_post_norm, w_in, ssm_conv_w, ssm_conv_b, ssm_dt_bias, ssm_A_log, ssm_D, ssm_norm, w_ssd_out, cc_conv_w, cc_conv_b, cc_ln_g, cc_ln_b, w_cc_out, w_o, ffn2_pre_norm, ffn2_post_norm, ffn2_w_gate, ffn2_w_up, ffn2_w_down):
    l = 0
    bf = lambda w: w.astype(BF16)
    npt, nst = N_PROMPT // FFN_TM, N_SAMPLE // FFN_TM

    f1 = (_row(ffn1_pre_norm[l]), _row(ffn1_post_norm[l]), _row(mix_pre_norm[l]),
          bf(ffn1_w_gate[l]), bf(ffn1_w_up[l]), bf(ffn1_w_down[l]))
    later = (ffn2_w_gate[l], ffn2_w_up[l], ffn2_w_down[l], w_in[l].T, w_ssd_out[l], w_cc_out[l], w_o[l])
    x1, h, w2g, w2u, w2d, wt, w_so, w_co, w_ob = _ffn(
        x_prompt.reshape(N_PROMPT, D_MODEL), 0, npt, *f1, out_rows=N_TOK, out_tile0=0, emit_h=True,
        casts=later, name="ffn1_prompt")
    x1, h = _ffn(x_sample.reshape(N_SAMPLE, D_MODEL), 0, nst, *f1, out_rows=N_TOK, out_tile0=npt,
                 emit_h=True, prev=(x1, h), name="ffn1_sample")

    c_xbc = DINNER
    c_dt = c_xbc + CONV_DIM
    c2 = c_dt + HEADS
    c3 = c2 + CC_DIM
    c4 = c3 + CC_DIM
    c5 = c4 + D_MODEL
    w_ga, w_gb = wt[c2:c3], wt[c3:c4]
    w_gs, w_gc = wt[c4:c5], wt[c5:]

    e01 = (jnp.arange(LANES)[:, None] == (jnp.arange(DINNER)[None, :] // HEADDIM)).astype(BF16)
    d_x = jnp.repeat(ssm_D[l].astype(F32), HEADDIM).reshape(1, DINNER)

    z = _mm(_mm_plain_kernel, h, [(wt, 0)], [], DINNER, F32, "proj_z", tn=WIDE_TN)
    xbc = _mm(_mm_plain_kernel, h, [(wt, c_xbc)], [], CONV_DIM, F32, "proj_xbc", tn=WIDE_TN)
    dt = _mm(_mm_dt_kernel, h, [(wt, c_dt)], [_pad_lanes(ssm_dt_bias[l])], LANES, F32, "proj_dt")
    u = _mm(_mm_glu_kernel, h, [(w_ga, 0), (w_gb, 0)], [], CC_DIM, F32, "proj_glu")

    ssm_params = (e01, _taps_on_sublanes(ssm_conv_w[l]), _row(ssm_conv_b[l]), _pad_lanes(ssm_A_log[l]),
                  d_x, _row(ssm_norm[l]))
    yn, ssm_p = _ssd_prompt(xbc, dt, z, *ssm_params)
    yn, ssm_s = _ssd_sample(yn, xbc.reshape(N_TOK // DEC_SEQ, DEC_SEQ, CONV_DIM), state_ssm_conv[l], dt, z,
                            state_ssm[l].reshape(DEC_BATCH, DINNER, STATE), *ssm_params)

    cc_params = (_taps_on_sublanes(cc_conv_w[l]), _row(cc_conv_b[l]), _row(cc_ln_g[l]), _row(cc_ln_b[l]))
    tq = 256
    uc = _cc_prompt(u.reshape(N_TOK // tq, tq, CC_DIM), *cc_params, tq=tq)
    uc = _cc_sample(uc, u.reshape(N_TOK // DEC_SEQ, DEC_SEQ, CC_DIM), state_cc_conv[l], *cc_params, n_seg=16)

    mg = _merge(h, yn, uc, w_gs, w_gc, w_so, w_co)
    x2 = _oproj(mg, x1, w_ob, _row(mix_post_norm[l]))

    f2 = (_row(ffn2_pre_norm[l]), _row(ffn2_post_norm[l]), _row(ffn2_post_norm[l]), w2g, w2u, w2d)
    (y_p,) = _ffn(x2, 0, npt, *f2, out_rows=N_PROMPT, out_tile0=0, emit_h=False, name="ffn2_prompt")
    (y_s,) = _ffn(x2, npt, nst, *f2, out_rows=N_SAMPLE, out_tile0=0, emit_h=False, name="ffn2_sample")

    y_prompt = y_p.reshape(BATCH, SEQ, D_MODEL)
    y_sample = y_s.reshape(DEC_BATCH, DEC_SEQ, D_MODEL)
    new_ssm_p = ssm_p.reshape(1, BATCH, HEADS, HEADDIM, STATE)
    new_ssm_s = ssm_s.reshape(1, DEC_BATCH, HEADS, HEADDIM, STATE)
    tail = lambda a, n: jnp.stack([a[(b + 1) * SEQ - n:(b + 1) * SEQ] for b in range(BATCH)])[None]
    new_sconv_p = tail(xbc, SSM_K - 1)
    new_cc_p = tail(u, CC_K - 1)
    xbc_s3 = xbc[N_PROMPT:].reshape(DEC_BATCH, DEC_SEQ, CONV_DIM)
    new_sconv_s = xbc_s3[:, DEC_SEQ - (SSM_K - 1):][None]
    u_s3 = u[N_PROMPT:].reshape(DEC_BATCH, DEC_SEQ, CC_DIM)
    new_cc_s = jnp.concatenate([state_cc_conv[l][:, DEC_SEQ:], u_s3], axis=1)[None]
    return (y_prompt, y_sample, new_ssm_p, new_sconv_p, new_cc_p, new_ssm_s, new_sconv_s, new_cc_s)
```

```python
import functools

import jax
import jax.numpy as jnp
from jax import lax
from jax.experimental import pallas as pl
from jax.experimental.pallas import tpu as pltpu

F32 = jnp.float32
BF16 = jnp.bfloat16

D_MODEL = 2048
D_FF = 5632
BATCH, SEQ = 4, 2048
DEC_BATCH, DEC_SEQ = 128, 8
N_PROMPT = BATCH * SEQ
N_SAMPLE = DEC_BATCH * DEC_SEQ
N_TOK = N_PROMPT + N_SAMPLE
HEADS, HEADDIM, GROUPS, STATE = 32, 64, 4, 128
HPG = HEADS // GROUPS
DINNER = HEADS * HEADDIM
GW = HPG * HEADDIM
SSM_K = 4
CONV_DIM = DINNER + 2 * GROUPS * STATE
CC_DIM = D_MODEL // 2
CC_K = 31
EPS = 1e-6

LANES = 128
SUBLANES = 8
CHUNK = 128
SEGS = CHUNK // DEC_SEQ
SEQ_PER_STEP = 4
VMEM_LIMIT = 48 * 1024 * 1024
FFN_VMEM_LIMIT = 58 * 1024 * 1024


def _cparams(sem, vmem=VMEM_LIMIT):
    return pltpu.CompilerParams(dimension_semantics=sem, vmem_limit_bytes=vmem)


def _rms(x, g):
    ms = jnp.mean(x * x, axis=-1, keepdims=True)
    return (x * lax.rsqrt(ms + EPS)) * g


def _dot(a, b):
    return jnp.dot(a, b, preferred_element_type=F32)


def _split3(x):
    hi = x.astype(BF16)
    r = x - hi.astype(F32)
    mid = r.astype(BF16)
    lo = (r - mid.astype(F32)).astype(BF16)
    return hi, mid, lo


def _dot01_rhs(x, m01x3):
    return _dot(jnp.concatenate(_split3(x), axis=1), m01x3)


def _dot01_lhs(m01, x):
    return _dot(jnp.concatenate([m01] * 3, axis=1), jnp.concatenate(_split3(x), axis=0))


BF16_ROWS = 16
NORM_ROWS = 8
FFN_TM = 512
FFN_TF = 512


def _ffn_kernel(*refs, n_prev, n_cast, emit_h):
    refs = refs[n_prev:]
    x_ref, pre_ref, post_ref, nxt_ref, wg_ref, wu_ref, wd_ref = refs[:7]
    src_refs = refs[7:7 + n_cast]
    outs = refs[7 + n_cast:]
    y_ref = outs[0]
    ho_ref = outs[1] if emit_h else None
    dst_refs = outs[1 + emit_h:1 + emit_h + n_cast]
    h_scr, acc_scr = outs[1 + emit_h + n_cast:]
    j = pl.program_id(1)

    @pl.when(j == 0)
    def _():
        h_scr[...] = _rms(x_ref[...], pre_ref[...]).astype(BF16)
        acc_scr[...] = jnp.zeros_like(acc_scr)

    for src_ref, dst_ref in zip(src_refs, dst_refs):
        dst_ref[...] = src_ref[...].astype(BF16)

    h = h_scr[...]
    g = _dot(h, wg_ref[...])
    u = _dot(h, wu_ref[...])
    a = (jax.nn.silu(g) * u).astype(BF16)
    acc_scr[...] += _dot(a, wd_ref[...])

    @pl.when(j == pl.num_programs(1) - 1)
    def _():
        if emit_h:
            xn = x_ref[...] + 0.5 * _rms(acc_scr[...], post_ref[...])
            y_ref[...] = xn
            ho_ref[...] = _rms(xn, nxt_ref[...]).astype(BF16)
        else:
            for r in range(0, x_ref.shape[0], NORM_ROWS):
                rows = slice(r, r + NORM_ROWS)
                y_ref[rows, :] = x_ref[rows, :] + 0.5 * _rms(acc_scr[rows, :], post_ref[...])


def _cast_chunk_rows(rows, n_steps):
    return BF16_ROWS * pl.cdiv(pl.cdiv(rows, BF16_ROWS), n_steps)


def _ffn(x, x_tile0, n_tiles, pre, post, nxt, wg, wu, wd, *, out_rows, out_tile0, emit_h,
         prev=(), casts=(), name, tm=FFN_TM, tf=FFN_TF):
    nj = D_FF // tf
    n_steps = n_tiles * nj
    vec = pl.BlockSpec((1, D_MODEL), lambda i, j: (0, 0))
    tile = lambda t0: pl.BlockSpec((tm, D_MODEL), functools.partial(lambda i, j, t: (t + i, 0), t=t0))
    out_shape = [jax.ShapeDtypeStruct((out_rows, D_MODEL), F32)]
    out_specs = [tile(out_tile0)]
    if emit_h:
        out_shape.append(jax.ShapeDtypeStruct((out_rows, D_MODEL), BF16))
        out_specs.append(tile(out_tile0))
    cast_specs = []
    for w, row0, rows in casts:
        cols = w.shape[1]
        rc = _cast_chunk_rows(rows, n_steps)
        assert row0 % rc == 0
        chunk = lambda i, j, first, last: (first + jnp.minimum(i * nj + j, last), 0)
        last = pl.cdiv(rows, rc) - 1
        cast_specs.append(pl.BlockSpec((rc, cols), functools.partial(chunk, first=row0 // rc, last=last)))
        out_shape.append(jax.ShapeDtypeStruct((rows, cols), BF16))
        out_specs.append(pl.BlockSpec((rc, cols), functools.partial(chunk, first=0, last=last)))
    return pl.pallas_call(
        functools.partial(_ffn_kernel, n_prev=len(prev), n_cast=len(casts), emit_h=emit_h),
        grid=(n_tiles, nj),
        in_specs=[pl.BlockSpec(memory_space=pl.ANY)] * len(prev) + [
            tile(x_tile0), vec, vec, vec,
            pl.BlockSpec((D_MODEL, tf), lambda i, j: (0, j)),
            pl.BlockSpec((D_MODEL, tf), lambda i, j: (0, j)),
            pl.BlockSpec((tf, D_MODEL), lambda i, j: (j, 0)),
        ] + cast_specs,
        out_specs=out_specs,
        out_shape=out_shape,
        scratch_shapes=[pltpu.VMEM((tm, D_MODEL), BF16), pltpu.VMEM((tm, D_MODEL), F32)],
        input_output_aliases={k: k for k in range(len(prev))},
        compiler_params=_cparams(("arbitrary", "arbitrary"), FFN_VMEM_LIMIT),
        name=name,
    )(*prev, x, pre, post, nxt, wg, wu, wd, *[w for w, _, _ in casts])


WIDE_TN = 1024


def _dot_nt(a, b_t):
    return lax.dot_general(a, b_t, (((1,), (1,)), ((), ())), preferred_element_type=F32)


def _mm_plain_kernel(h_ref, w_ref, o_ref):
    o_ref[...] = _dot_nt(h_ref[...], w_ref[...]).astype(o_ref.dtype)


def _z_dt_kernel(h_ref, wz_ref, wdt_ref, b_ref, z_ref, dt_ref):
    h = h_ref[...]
    z_ref[...] = _dot_nt(h, wz_ref[...])

    @pl.when(pl.program_id(1) == 0)
    def _():
        dt = jax.nn.softplus(_dot_nt(h, wdt_ref[...]) + b_ref[...])
        lane = lax.broadcasted_iota(jnp.int32, dt.shape, 1)
        dt_ref[...] = jnp.where(lane < HEADS, dt, 0.0)


def _proj_z_dt(h, wt, dt_row0, dt_bias, *, tm=1024, tn=WIDE_TN):
    m, k = h.shape
    assert dt_row0 % LANES == 0
    return pl.pallas_call(
        _z_dt_kernel,
        grid=(m // tm, DINNER // tn),
        in_specs=[
            pl.BlockSpec((tm, k), lambda i, j: (i, 0)),
            pl.BlockSpec((tn, k), lambda i, j: (j, 0)),
            pl.BlockSpec((LANES, k), lambda i, j: (dt_row0 // LANES, 0)),
            pl.BlockSpec((1, LANES), lambda i, j: (0, 0)),
        ],
        out_specs=[
            pl.BlockSpec((tm, tn), lambda i, j: (i, j)),
            pl.BlockSpec((tm, LANES), lambda i, j: (i, 0)),
        ],
        out_shape=[jax.ShapeDtypeStruct((m, DINNER), F32), jax.ShapeDtypeStruct((m, LANES), F32)],
        compiler_params=_cparams(("parallel", "arbitrary")),
        name="proj_z_dt",
    )(h, wt, wt, dt_bias)


def _mm_glu_kernel(h_ref, wa_ref, wb_ref, o_ref):
    h = h_ref[...]
    o_ref[...] = _dot_nt(h, wa_ref[...]) * jax.nn.sigmoid(_dot_nt(h, wb_ref[...]))


def _mm(kernel, h, ws, n_out, out_dtype, name, *, tm=1024, tn=512):
    m, k = h.shape
    tn = min(tn, n_out)
    grid = (m // tm, n_out // tn)
    in_specs = [pl.BlockSpec((tm, k), lambda i, j: (i, 0))]
    for _, r0 in ws:
        assert r0 % tn == 0
        in_specs.append(pl.BlockSpec((tn, k), functools.partial(lambda i, j, o: (o + j, 0), o=r0 // tn)))
    return pl.pallas_call(
        kernel,
        grid=grid,
        in_specs=in_specs,
        out_specs=pl.BlockSpec((tm, tn), lambda i, j: (i, j)),
        out_shape=jax.ShapeDtypeStruct((m, n_out), out_dtype),
        compiler_params=_cparams(("parallel", "arbitrary")),
        name=name,
    )(h, *[w for w, _ in ws])


def _block_masks(seg_len):
    t = lax.broadcasted_iota(jnp.int32, (CHUNK, CHUNK), 0)
    s = lax.broadcasted_iota(jnp.int32, (CHUNK, CHUNK), 1)
    if seg_len == CHUNK:
        same = t >= 0
    else:
        sh = seg_len.bit_length() - 1
        same = (t >> sh) == (s >> sh)
    causal = jnp.logical_and(same, s <= t)
    return same, causal


CONV_STRIP = 512


def _conv4_silu(full, w8_ref, cb_ref, cols):
    n = full.shape[-2] - SUBLANES
    width = full.shape[-1]
    acc = None
    for k in range(SSM_K):
        win = full[..., 5 + k:5 + k + n, :].reshape(CHUNK // SUBLANES, SUBLANES, width)
        t = w8_ref[k, :, cols][None] * win
        acc = t if acc is None else acc + t
    return jax.nn.silu(acc.reshape(CHUNK, width) + cb_ref[:, cols])


def _conv4_strips(load_strip, w8_ref, cb_ref):
    strips = []
    for c0 in range(0, CONV_DIM, CONV_STRIP):
        cols = slice(c0, c0 + CONV_STRIP)
        strips.append(_conv4_silu(load_strip(cols), w8_ref, cb_ref, cols))
    return strips


def _ssd_intra(strips, dt, alog_row, e01, seg_len):
    same, causal = _block_masks(seg_len)
    n_x = DINNER // CONV_STRIP
    xs = jnp.concatenate(strips[:n_x], axis=1)
    bm = strips[n_x]
    cm = strips[n_x + 1]
    a_row = -jnp.exp(alog_row)
    dta = dt * a_row
    causal01 = jnp.where(causal, 1.0, 0.0).astype(BF16)
    same01 = jnp.where(same, 1.0, 0.0).astype(BF16)
    cum = _dot01_lhs(causal01, dta)
    cum_end = _dot01_lhs(same01, dta)
    cum_t = cum.T
    dt_t = dt.T
    ecum_x = _dot01_rhs(jnp.exp(cum), e01)
    coef_x = _dot01_rhs(jnp.exp(cum_end - cum) * dt, e01)

    lane = lax.broadcasted_iota(jnp.int32, (CHUNK, LANES), 1)
    first_head = lane < HEADDIM
    ys = []
    for g in range(GROUPS):
        cg = cm[:, g * STATE:(g + 1) * STATE].astype(BF16)
        bg = bm[:, g * STATE:(g + 1) * STATE].astype(BF16)
        cb = lax.dot_general(cg, bg, (((1,), (1,)), ((), ())), preferred_element_type=F32)
        for p in range(HPG // 2):
            ws = []
            for e in (g * HPG + 2 * p, g * HPG + 2 * p + 1):
                seg = cum[:, e:e + 1] - cum_t[e:e + 1, :]
                dec = jnp.exp(jnp.where(causal, seg, -jnp.inf))
                ws.append(((cb * dec) * dt_t[e:e + 1, :]).astype(BF16))
            w2 = jnp.concatenate(ws, axis=1)
            col = (g * HPG + 2 * p) * HEADDIM
            x2 = xs[:, col:col + LANES]
            r2 = jnp.concatenate([jnp.where(first_head, x2, 0.0),
                                  jnp.where(first_head, 0.0, x2)], axis=0).astype(BF16)
            ys.append(_dot(w2, r2))
    y_diag = jnp.concatenate(ys, axis=1)
    return xs, cm, bm, y_diag, coef_x, ecum_x


def _store_gated_norm(yn_ref, y, z_ref, nrm_ref):
    for r in range(0, CHUNK, BF16_ROWS):
        rows = slice(r, r + BF16_ROWS)
        yn_ref[rows, :] = _rms(y[rows, :] * jax.nn.silu(z_ref[rows, :]), nrm_ref[...]).astype(BF16)


def _ssd_prompt_kernel(xbc_ref, dt_ref, z_ref, w8_ref, cb_ref, e_ref, alog_ref, dx_ref, nrm_ref,
                       yn_ref, hout_ref, xp_ref, ht_ref):
    c = pl.program_id(1)

    @pl.when(c == 0)
    def _():
        xp_ref[0:SUBLANES, :] = jnp.zeros((SUBLANES, CONV_DIM), F32)
        ht_ref[...] = jnp.zeros_like(ht_ref)

    xp_ref[SUBLANES:SUBLANES + CHUNK, :] = xbc_ref[...]
    strips = _conv4_strips(lambda cols: xp_ref[:, cols], w8_ref, cb_ref)
    xp_ref[0:SUBLANES, :] = xbc_ref[CHUNK - SUBLANES:CHUNK, :]

    xs, cm, bm, y_diag, coef_x, ecum_x = _ssd_intra(strips, dt_ref[...], alog_ref[...], e_ref[...], CHUNK)
    xcoef = (xs * coef_x).astype(BF16)
    dec_row = ecum_x[CHUNK - 1:CHUNK, :]
    y_off = []
    for g in range(GROUPS):
        gs = slice(g * GW, (g + 1) * GW)
        ht_g = ht_ref[:, gs]
        cg = cm[:, g * STATE:(g + 1) * STATE].astype(BF16)
        y_off.append(_dot(cg, ht_g.astype(BF16)))
        bg_t = bm[:, g * STATE:(g + 1) * STATE].T.astype(BF16)
        ht_ref[:, gs] = ht_g * dec_row[:, gs] + _dot(bg_t, xcoef[:, gs])
    y = y_diag + jnp.concatenate(y_off, axis=1) * ecum_x + dx_ref[...] * xs
    _store_gated_norm(yn_ref, y, z_ref, nrm_ref)

    @pl.when(c == pl.num_programs(1) - 1)
    def _():
        hout_ref[...] = ht_ref[...].T


def _ssm_param_specs(const):
    return [
        pl.BlockSpec((SSM_K, SUBLANES, CONV_DIM), lambda *_: (0, 0, 0)),
        pl.BlockSpec((1, CONV_DIM), const),
        pl.BlockSpec((3 * LANES, DINNER), const),
        pl.BlockSpec((1, LANES), const),
        pl.BlockSpec((1, DINNER), const),
        pl.BlockSpec((1, DINNER), const),
    ]


def _ssd_prompt(xbc, dt, z, *params):
    nc = SEQ // CHUNK
    blk = lambda b, c: (b * nc + c, 0)
    const = lambda b, c: (0, 0)
    return pl.pallas_call(
        _ssd_prompt_kernel,
        grid=(BATCH, nc),
        in_specs=[
            pl.BlockSpec((CHUNK, CONV_DIM), blk),
            pl.BlockSpec((CHUNK, LANES), blk),
            pl.BlockSpec((CHUNK, DINNER), blk),
        ] + _ssm_param_specs(const),
        out_specs=[
            pl.BlockSpec((CHUNK, DINNER), blk),
            pl.BlockSpec((None, DINNER, STATE), lambda b, c: (b, 0, 0)),
        ],
        out_shape=[
            jax.ShapeDtypeStruct((N_TOK, DINNER), BF16),
            jax.ShapeDtypeStruct((BATCH, DINNER, STATE), F32),
        ],
        scratch_shapes=[pltpu.VMEM((CHUNK + SUBLANES, CONV_DIM), F32), pltpu.VMEM((STATE, DINNER), F32)],
        compiler_params=_cparams(("parallel", "arbitrary")),
        name="ssd_prompt",
    )(xbc, dt, z, *params)


def _ssd_sample_kernel(yn_any, xbc_ref, cst_ref, dt_ref, z_ref, h0_ref, w8_ref, cb_ref, e_ref,
                       alog_ref, dx_ref, nrm_ref, yn_ref, hout_ref,
                       xp_ref, cm_ref, bt_ref, xcoef_ref, ecx_ref, y_ref):
    del yn_any
    j = pl.program_id(1)

    @pl.when(j == 0)
    def _():
        xp_ref[:, SUBLANES - (SSM_K - 1):SUBLANES, :] = cst_ref[...]
        xp_ref[:, SUBLANES:SUBLANES + DEC_SEQ, :] = xbc_ref[...]
        strips = _conv4_strips(lambda cols: xp_ref[:, :, cols], w8_ref, cb_ref)
        xs, cm, bm, y_diag, coef_x, ecum_x = _ssd_intra(
            strips, dt_ref[...], alog_ref[...], e_ref[...], DEC_SEQ)
        cm_ref[...] = cm
        for g in range(GROUPS):
            bt_ref[g] = bm[:, g * STATE:(g + 1) * STATE].T
        xcoef_ref[...] = (xs * coef_x).astype(BF16)
        ecx_ref[...] = ecum_x
        y_ref[...] = y_diag + dx_ref[...] * xs

    col = lax.broadcasted_iota(jnp.int32, (STATE, CHUNK), 1)
    base = j * (SEQ_PER_STEP * DEC_SEQ)
    for p in range(SEQ_PER_STEP):
        r0 = pl.multiple_of(base + p * DEC_SEQ, DEC_SEQ)
        rows = pl.ds(r0, DEC_SEQ)
        ht = h0_ref[p].T
        ht_b = ht.astype(BF16)
        ecx = ecx_ref[rows, :]
        cj = cm_ref[rows, :].astype(BF16)
        mine = jnp.logical_and(col >= r0, col < r0 + DEC_SEQ)
        y_off, new = [], []
        for g in range(GROUPS):
            gs = slice(g * GW, (g + 1) * GW)
            y_off.append(_dot(cj[:, g * STATE:(g + 1) * STATE], ht_b[:, gs]))
            bg_t = jnp.where(mine, bt_ref[g], 0.0).astype(BF16)
            new.append(_dot(bg_t, xcoef_ref[:, gs]))
        y_ref[rows, :] += jnp.concatenate(y_off, axis=1) * ecx
        ht_new = ht * ecx[DEC_SEQ - 1:DEC_SEQ, :] + jnp.concatenate(new, axis=1)
        hout_ref[p] = ht_new.T

    @pl.when(j == pl.num_programs(1) - 1)
    def _():
        _store_gated_norm(yn_ref, y_ref, z_ref, nrm_ref)


def _ssd_sample(yn_all, xbc3, cst, dt, z, h0, *params):
    nb = DEC_BATCH // SEGS
    off = N_PROMPT // CHUNK
    const = lambda i, j: (0, 0)
    tok = lambda i, j: (off + i, 0)
    seq = lambda i, j: (i * (SEGS // SEQ_PER_STEP) + j, 0, 0)
    return pl.pallas_call(
        _ssd_sample_kernel,
        grid=(nb, SEGS // SEQ_PER_STEP),
        in_specs=[
            pl.BlockSpec(memory_space=pl.ANY),
            pl.BlockSpec((SEGS, DEC_SEQ, CONV_DIM), lambda i, j: (off + i, 0, 0)),
            pl.BlockSpec((SEGS, SSM_K - 1, CONV_DIM), lambda i, j: (i, 0, 0)),
            pl.BlockSpec((CHUNK, LANES), tok),
            pl.BlockSpec((CHUNK, DINNER), tok),
            pl.BlockSpec((SEQ_PER_STEP, DINNER, STATE), seq),
        ] + _ssm_param_specs(const),
        out_specs=[
            pl.BlockSpec((CHUNK, DINNER), tok),
            pl.BlockSpec((SEQ_PER_STEP, DINNER, STATE), seq),
        ],
        out_shape=[
            jax.ShapeDtypeStruct((N_TOK, DINNER), BF16),
            jax.ShapeDtypeStruct((DEC_BATCH, DINNER, STATE), F32),
        ],
        scratch_shapes=[
            pltpu.VMEM((SEGS, SUBLANES + DEC_SEQ, CONV_DIM), F32),
            pltpu.VMEM((CHUNK, GROUPS * STATE), F32),
            pltpu.VMEM((GROUPS, STATE, CHUNK), F32),
            pltpu.VMEM((CHUNK, DINNER), BF16),
            pltpu.VMEM((CHUNK, DINNER), F32),
            pltpu.VMEM((CHUNK, DINNER), F32),
        ],
        input_output_aliases={0: 0},
        compiler_params=_cparams(("parallel", "arbitrary")),
        name="ssd_sample",
    )(yn_all, xbc3, cst, dt, z, h0, *params)


CC_PAD = 32
CC_FIRST = CC_PAD - (CC_K - 1)
CC_LN_ROWS = 32
CC_STRIP = 512


def _cc_kernel(*refs, n_seg, seg_len, carry, aliased):
    refs = list(refs)
    if aliased:
        refs.pop(0)
    u_ref = refs.pop(0)
    st_ref = None if carry else refs.pop(0)
    w8_ref, b_ref, g_ref, be_ref, o_ref, xp_ref, res_ref = refs
    if carry:
        @pl.when(pl.program_id(1) == 0)
        def _():
            xp_ref[:, 0:CC_PAD, :] = jnp.zeros((n_seg, CC_PAD, CC_DIM), F32)
    else:
        xp_ref[:, CC_FIRST:CC_PAD, :] = st_ref[...]
    xp_ref[:, CC_PAD:CC_PAD + seg_len, :] = u_ref[...]

    rb = min(seg_len, 64)
    n_shift = rb + CC_PAD - SUBLANES
    for s in range(n_seg):
        for t0 in range(0, seg_len, rb):
            for c0 in range(0, CC_DIM, CC_STRIP):
                cols = slice(c0, c0 + CC_STRIP)
                full = xp_ref[s, t0:t0 + rb + CC_PAD, cols]
                acc = None
                for b in range(SUBLANES):
                    xb = full if b == 0 else full[b:b + n_shift, :]
                    for a in range(CC_PAD // SUBLANES + 1):
                        k = SUBLANES * a + b - CC_FIRST
                        if k < 0 or k >= CC_K:
                            continue
                        win = xb[SUBLANES * a:SUBLANES * a + rb, :].reshape(rb // SUBLANES, SUBLANES, CC_STRIP)
                        t = w8_ref[k, :, cols][None] * win
                        acc = t if acc is None else acc + t
                row = s * seg_len + t0
                res_ref[row:row + rb, cols] = acc.reshape(rb, CC_STRIP) + b_ref[:, cols]

    for r0 in range(0, n_seg * seg_len, CC_LN_ROWS):
        v = res_ref[r0:r0 + CC_LN_ROWS, :]
        mu = jnp.mean(v, axis=-1, keepdims=True)
        d = v - mu
        var = jnp.mean(d * d, axis=-1, keepdims=True)
        yv = (d * lax.rsqrt(var + EPS)) * g_ref[...] + be_ref[...]
        o_ref[r0:r0 + CC_LN_ROWS, :] = jax.nn.silu(yv).astype(BF16)
    if carry:
        xp_ref[:, 0:CC_PAD, :] = xp_ref[:, seg_len:seg_len + CC_PAD, :]


def _cc_param_specs(const):
    return [
        pl.BlockSpec((CC_K, SUBLANES, CC_DIM), lambda *_: (0, 0, 0)),
        pl.BlockSpec((1, CC_DIM), const),
        pl.BlockSpec((1, CC_DIM), const),
        pl.BlockSpec((1, CC_DIM), const),
    ]


def _cc_prompt(u3, *params, tq):
    nc = SEQ // tq
    const = lambda i, c: (0, 0)
    return pl.pallas_call(
        functools.partial(_cc_kernel, n_seg=1, seg_len=tq, carry=True, aliased=False),
        grid=(BATCH, nc),
        in_specs=[pl.BlockSpec((1, tq, CC_DIM), lambda i, c: (i * nc + c, 0, 0))] + _cc_param_specs(const),
        out_specs=pl.BlockSpec((tq, CC_DIM), lambda i, c: (i * nc + c, 0)),
        out_shape=jax.ShapeDtypeStruct((N_TOK, CC_DIM), BF16),
        scratch_shapes=[pltpu.VMEM((1, CC_PAD + tq, CC_DIM), F32), pltpu.VMEM((tq, CC_DIM), F32)],
        compiler_params=_cparams(("parallel", "arbitrary")),
        name="cc_prompt",
    )(u3, *params)


def _cc_sample(uc_all, u3, st, *params, n_seg):
    const = lambda i: (0, 0)
    rows = n_seg * DEC_SEQ
    off = N_PROMPT // rows
    return pl.pallas_call(
        functools.partial(_cc_kernel, n_seg=n_seg, seg_len=DEC_SEQ, carry=False, aliased=True),
        grid=(DEC_BATCH // n_seg,),
        in_specs=[
            pl.BlockSpec(memory_space=pl.ANY),
            pl.BlockSpec((n_seg, DEC_SEQ, CC_DIM), lambda i: (off + i, 0, 0)),
            pl.BlockSpec((n_seg, CC_K - 1, CC_DIM), lambda i: (i, 0, 0)),
        ] + _cc_param_specs(const),
        out_specs=pl.BlockSpec((rows, CC_DIM), lambda i: (off + i, 0)),
        out_shape=jax.ShapeDtypeStruct((N_TOK, CC_DIM), BF16),
        scratch_shapes=[pltpu.VMEM((n_seg, CC_PAD + DEC_SEQ, CC_DIM), F32),
                        pltpu.VMEM((rows, CC_DIM), F32)],
        input_output_aliases={0: 0},
        compiler_params=_cparams(("parallel",)),
        name="cc_sample",
    )(uc_all, u3, st, *params)


def _merge_kernel(h_ref, yn_ref, uc_ref, wgs_ref, wgc_ref, wso_ref, wco_ref, o_ref):
    h = h_ref[...]
    g_ssd = jax.nn.sigmoid(_dot_nt(h, wgs_ref[...]))
    g_cc = jax.nn.sigmoid(_dot_nt(h, wgc_ref[...]))
    ssd_out = _dot(yn_ref[...], wso_ref[...])
    cc_out = _dot(uc_ref[...], wco_ref[...])
    o_ref[...] = (g_ssd * ssd_out + g_cc * cc_out).astype(BF16)


def _merge(h, yn, uc, wgate_t, wso, wco, *, tm=1024, tn=512):
    m = h.shape[0]
    row = lambda i, j: (i, 0)
    colw = lambda i, j: (0, j)
    roww = lambda i, j: (j, 0)
    roww2 = lambda i, j: (D_MODEL // tn + j, 0)
    return pl.pallas_call(
        _merge_kernel,
        grid=(m // tm, D_MODEL // tn),
        in_specs=[
            pl.BlockSpec((tm, D_MODEL), row),
            pl.BlockSpec((tm, DINNER), row),
            pl.BlockSpec((tm, CC_DIM), row),
            pl.BlockSpec((tn, D_MODEL), roww),
            pl.BlockSpec((tn, D_MODEL), roww2),
            pl.BlockSpec((DINNER, tn), colw),
            pl.BlockSpec((CC_DIM, tn), colw),
        ],
        out_specs=pl.BlockSpec((tm, tn), lambda i, j: (i, j)),
        out_shape=jax.ShapeDtypeStruct((m, D_MODEL), BF16),
        compiler_params=_cparams(("parallel", "arbitrary")),
        name="merge",
    )(h, yn, uc, wgate_t, wgate_t, wso, wco)


def _oproj_kernel(m_ref, x_ref, w_ref, nrm_ref, o_ref):
    o_ref[...] = x_ref[...] + _rms(_dot(m_ref[...], w_ref[...]), nrm_ref[...])


def _oproj(mg, x, w, nrm, *, tm=512):
    m = x.shape[0]
    row = lambda i: (i, 0)
    const = lambda i: (0, 0)
    return pl.pallas_call(
        _oproj_kernel,
        grid=(m // tm,),
        in_specs=[
            pl.BlockSpec((tm, D_MODEL), row),
            pl.BlockSpec((tm, D_MODEL), row),
            pl.BlockSpec((D_MODEL, D_MODEL), const),
            pl.BlockSpec((1, D_MODEL), const),
        ],
        out_specs=pl.BlockSpec((tm, D_MODEL), row),
        out_shape=jax.ShapeDtypeStruct((m, D_MODEL), F32),
        compiler_params=_cparams(("parallel",)),
        name="oproj",
    )(mg, x, w, nrm)


def _row(v):
    return v.reshape(1, -1).astype(F32)


def _pad_lanes(v):
    return jnp.pad(v.astype(F32), (0, LANES - v.shape[0])).reshape(1, LANES)


def _taps_on_sublanes(w):
    return jnp.broadcast_to(w.astype(F32)[:, None, :], (w.shape[0], SUBLANES, w.shape[1]))


def kernel(x_prompt, x_sample, state_ssm, state_ssm_conv, state_cc_conv, ffn1_pre_norm, ffn1_post_norm, ffn1_w_gate, ffn1_w_up, ffn1_w_down, mix_pre_norm, mix_post_norm, w_in, ssm_conv_w, ssm_conv_b, ssm_dt_bias, ssm_A_log, ssm_D, ssm_norm, w_ssd_out, cc_conv_w, cc_conv_b, cc_ln_g, cc_ln_b, w_cc_out, w_o, ffn2_pre_norm, ffn2_post_norm, ffn2_w_gate, ffn2_w_up, ffn2_w_down):
    l = 0
    bf = lambda w: w.astype(BF16)
    npt, nst = N_PROMPT // FFN_TM, N_SAMPLE // FFN_TM

    f1 = (_row(ffn1_pre_norm[l]), _row(ffn1_post_norm[l]), _row(mix_pre_norm[l]),
          bf(ffn1_w_gate[l]), bf(ffn1_w_up[l]), bf(ffn1_w_down[l]))
    w_in_t = w_in[l].T
    c_xbc = DINNER
    c_dt = c_xbc + CONV_DIM
    c_glu = c_dt + HEADS
    c_gate = c_glu + 2 * CC_DIM
    whole = lambda w: (w, 0, w.shape[0])
    later = (whole(ffn2_w_gate[l]), whole(ffn2_w_up[l]), whole(ffn2_w_down[l]),
             (w_in_t, 0, c_dt + LANES), (w_in_t, c_glu, 2 * CC_DIM), (w_in_t, c_gate, 2 * D_MODEL),
             whole(w_ssd_out[l]), whole(w_cc_out[l]), whole(w_o[l]))
    x1, h, w2g, w2u, w2d, wt, wt_glu, wt_gate, w_so, w_co, w_ob = _ffn(
        x_prompt.reshape(N_PROMPT, D_MODEL), 0, npt, *f1, out_rows=N_TOK, out_tile0=0, emit_h=True,
        casts=later, name="ffn1_prompt")
    x1, h = _ffn(x_sample.reshape(N_SAMPLE, D_MODEL), 0, nst, *f1, out_rows=N_TOK, out_tile0=npt,
                 emit_h=True, prev=(x1, h), name="ffn1_sample")

    e01 = (jnp.arange(LANES)[:, None] == (jnp.arange(DINNER)[None, :] // HEADDIM)).astype(BF16)
    e01 = jnp.concatenate([e01] * 3, axis=0)
    d_x = jnp.repeat(ssm_D[l].astype(F32), HEADDIM).reshape(1, DINNER)

    z, dt = _proj_z_dt(h, wt, c_dt, _pad_lanes(ssm_dt_bias[l]))
    xbc = _mm(_mm_plain_kernel, h, [(wt, c_xbc)], CONV_DIM, F32, "proj_xbc", tn=WIDE_TN)
    u = _mm(_mm_glu_kernel, h, [(wt_glu, 0), (wt_glu, CC_DIM)], CC_DIM, F32, "proj_glu")

    ssm_params = (_taps_on_sublanes(ssm_conv_w[l]), _row(ssm_conv_b[l]), e01, _pad_lanes(ssm_A_log[l]),
                  d_x, _row(ssm_norm[l]))
    yn, ssm_p = _ssd_prompt(xbc, dt, z, *ssm_params)
    yn, ssm_s = _ssd_sample(yn, xbc.reshape(N_TOK // DEC_SEQ, DEC_SEQ, CONV_DIM), state_ssm_conv[l], dt, z,
                            state_ssm[l].reshape(DEC_BATCH, DINNER, STATE), *ssm_params)

    cc_params = (_taps_on_sublanes(cc_conv_w[l]), _row(cc_conv_b[l]), _row(cc_ln_g[l]), _row(cc_ln_b[l]))
    tq = 256
    uc = _cc_prompt(u.reshape(N_TOK // tq, tq, CC_DIM), *cc_params, tq=tq)
    uc = _cc_sample(uc, u.reshape(N_TOK // DEC_SEQ, DEC_SEQ, CC_DIM), state_cc_conv[l], *cc_params, n_seg=16)

    mg = _merge(h, yn, uc, wt_gate, w_so, w_co)
    x2 = _oproj(mg, x1, w_ob, _row(mix_post_norm[l]))

    f2 = (_row(ffn2_pre_norm[l]), _row(ffn2_post_norm[l]), _row(ffn2_post_norm[l]), w2g, w2u, w2d)
    (y_p,) = _ffn(x2, 0, npt, *f2, out_rows=N_PROMPT, out_tile0=0, emit_h=False, name="ffn2_prompt")
    (y_s,) = _ffn(x2, npt, nst, *f2, out_rows=N_SAMPLE, out_tile0=0, emit_h=False, name="ffn2_sample")

    y_prompt = y_p.reshape(BATCH, SEQ, D_MODEL)
    y_sample = y_s.reshape(DEC_BATCH, DEC_SEQ, D_MODEL)
    new_ssm_p = ssm_p.reshape(1, BATCH, HEADS, HEADDIM, STATE)
    new_ssm_s = ssm_s.reshape(1, DEC_BATCH, HEADS, HEADDIM, STATE)
    tail = lambda a, n: jnp.stack([a[(b + 1) * SEQ - n:(b + 1) * SEQ] for b in range(BATCH)])[None]
    new_sconv_p = tail(xbc, SSM_K - 1)
    new_cc_p = tail(u, CC_K - 1)
    xbc_s3 = xbc[N_PROMPT:].reshape(DEC_BATCH, DEC_SEQ, CONV_DIM)
    new_sconv_s = xbc_s3[:, DEC_SEQ - (SSM_K - 1):][None]
    u_s3 = u[N_PROMPT:].reshape(DEC_BATCH, DEC_SEQ, CC_DIM)
    new_cc_s = jnp.concatenate([state_cc_conv[l][:, DEC_SEQ:], u_s3], axis=1)[None]
    return (y_prompt, y_sample, new_ssm_p, new_sconv_p, new_cc_p, new_ssm_s, new_sconv_s, new_cc_s)
```

```python
import functools

import jax
import jax.numpy as jnp
from jax import lax
from jax.experimental import pallas as pl
from jax.experimental.pallas import tpu as pltpu

F32 = jnp.float32
BF16 = jnp.bfloat16

D_MODEL = 2048
D_FF = 5632
BATCH, SEQ = 4, 2048
DEC_BATCH, DEC_SEQ = 128, 8
N_PROMPT = BATCH * SEQ
N_SAMPLE = DEC_BATCH * DEC_SEQ
N_TOK = N_PROMPT + N_SAMPLE
HEADS, HEADDIM, GROUPS, STATE = 32, 64, 4, 128
HPG = HEADS // GROUPS
DINNER = HEADS * HEADDIM
GW = HPG * HEADDIM
SSM_K = 4
CONV_DIM = DINNER + 2 * GROUPS * STATE
CC_DIM = D_MODEL // 2
CC_K = 31
EPS = 1e-6

LANES = 128
SUBLANES = 8
CHUNK = 128
SEGS = CHUNK // DEC_SEQ
SEQ_PER_STEP = 4
VMEM_LIMIT = 48 * 1024 * 1024
FFN_VMEM_LIMIT = 58 * 1024 * 1024


def _cparams(sem, vmem=VMEM_LIMIT):
    return pltpu.CompilerParams(dimension_semantics=sem, vmem_limit_bytes=vmem)


def _rms(x, g):
    ms = jnp.mean(x * x, axis=-1, keepdims=True)
    return (x * lax.rsqrt(ms + EPS)) * g


def _dot(a, b):
    return jnp.dot(a, b, preferred_element_type=F32)


def _split3(x):
    hi = x.astype(BF16)
    r = x - hi.astype(F32)
    mid = r.astype(BF16)
    lo = (r - mid.astype(F32)).astype(BF16)
    return hi, mid, lo


def _dot01_rhs(x, m01x3):
    return _dot(jnp.concatenate(_split3(x), axis=1), m01x3)


def _dot01_lhs(m01, x):
    return _dot(jnp.concatenate([m01] * 3, axis=1), jnp.concatenate(_split3(x), axis=0))


BF16_ROWS = 16
NORM_ROWS = 8
FFN_TM = 512
FFN_TF = 512


def _ffn_kernel(*refs, n_prev, n_cast, emit_h, f32_weights):
    refs = refs[n_prev:]
    x_ref, pre_ref, post_ref, nxt_ref, wg_ref, wu_ref, wd_ref = refs[:7]
    src_refs = refs[7:7 + n_cast]
    outs = refs[7 + n_cast:]
    y_ref = outs[0]
    ho_ref = outs[1] if emit_h else None
    dst_refs = outs[1 + emit_h:1 + emit_h + n_cast]
    own_refs = outs[1 + emit_h + n_cast:-2]
    h_scr, acc_scr = outs[-2:]
    j = pl.program_id(1)

    @pl.when(j == 0)
    def _():
        h_scr[...] = _rms(x_ref[...], pre_ref[...]).astype(BF16)
        acc_scr[...] = jnp.zeros_like(acc_scr)

    _cast_side_job(src_refs, dst_refs)

    wg, wu, wd = wg_ref[...], wu_ref[...], wd_ref[...]
    if f32_weights:
        wg, wu, wd = wg.astype(BF16), wu.astype(BF16), wd.astype(BF16)
        for own_ref, w in zip(own_refs, (wg, wu, wd)):
            own_ref[...] = w
    h = h_scr[...]
    g = _dot(h, wg)
    u = _dot(h, wu)
    a = (jax.nn.silu(g) * u).astype(BF16)
    acc_scr[...] += _dot(a, wd)

    @pl.when(j == pl.num_programs(1) - 1)
    def _():
        if emit_h:
            xn = x_ref[...] + 0.5 * _rms(acc_scr[...], post_ref[...])
            y_ref[...] = xn
            ho_ref[...] = _rms(xn, nxt_ref[...]).astype(BF16)
        else:
            for r in range(0, x_ref.shape[0], NORM_ROWS):
                rows = slice(r, r + NORM_ROWS)
                y_ref[rows, :] = x_ref[rows, :] + 0.5 * _rms(acc_scr[rows, :], post_ref[...])


def _cast_chunk_rows(rows, n_steps):
    return BF16_ROWS * pl.cdiv(pl.cdiv(rows, BF16_ROWS), n_steps)


def _cast_streams(casts, n_outer, n_inner):
    in_specs, out_specs, out_shape = [], [], []
    for w, row0, rows in casts:
        cols = w.shape[1]
        rc = _cast_chunk_rows(rows, n_outer * n_inner)
        assert row0 % rc == 0
        chunk = lambda i, j, first, last: (first + jnp.minimum(i * n_inner + j, last), 0)
        last = pl.cdiv(rows, rc) - 1
        in_specs.append(pl.BlockSpec((rc, cols), functools.partial(chunk, first=row0 // rc, last=last)))
        out_specs.append(pl.BlockSpec((rc, cols), functools.partial(chunk, first=0, last=last)))
        out_shape.append(jax.ShapeDtypeStruct((rows, cols), BF16))
    return in_specs, out_specs, out_shape


def _cast_side_job(src_refs, dst_refs):
    for src_ref, dst_ref in zip(src_refs, dst_refs):
        dst_ref[...] = src_ref[...].astype(BF16)


def _ffn(x, x_tile0, n_tiles, pre, post, nxt, wg, wu, wd, *, out_rows, out_tile0, emit_h,
         prev=(), casts=(), name, tm=FFN_TM, tf=FFN_TF):
    nj = D_FF // tf
    vec = pl.BlockSpec((1, D_MODEL), lambda i, j: (0, 0))
    tile = lambda t0: pl.BlockSpec((tm, D_MODEL), functools.partial(lambda i, j, t: (t + i, 0), t=t0))
    out_shape = [jax.ShapeDtypeStruct((out_rows, D_MODEL), F32)]
    out_specs = [tile(out_tile0)]
    if emit_h:
        out_shape.append(jax.ShapeDtypeStruct((out_rows, D_MODEL), BF16))
        out_specs.append(tile(out_tile0))
    cast_specs, cast_out_specs, cast_out_shape = _cast_streams(casts, n_tiles, nj)
    out_specs += cast_out_specs
    out_shape += cast_out_shape
    w_specs = [
        pl.BlockSpec((D_MODEL, tf), lambda i, j: (0, j)),
        pl.BlockSpec((D_MODEL, tf), lambda i, j: (0, j)),
        pl.BlockSpec((tf, D_MODEL), lambda i, j: (j, 0)),
    ]
    f32_weights = wg.dtype == F32
    if f32_weights:
        once = lambda i, j: jnp.where(i == 0, j, nj - 1)
        out_shape += [jax.ShapeDtypeStruct(w.shape, BF16) for w in (wg, wu, wd)]
        out_specs += [
            pl.BlockSpec((D_MODEL, tf), lambda i, j: (0, once(i, j))),
            pl.BlockSpec((D_MODEL, tf), lambda i, j: (0, once(i, j))),
            pl.BlockSpec((tf, D_MODEL), lambda i, j: (once(i, j), 0)),
        ]
    return pl.pallas_call(
        functools.partial(_ffn_kernel, n_prev=len(prev), n_cast=len(casts), emit_h=emit_h,
                          f32_weights=f32_weights),
        grid=(n_tiles, nj),
        in_specs=[pl.BlockSpec(memory_space=pl.ANY)] * len(prev)
        + [tile(x_tile0), vec, vec, vec] + w_specs + cast_specs,
        out_specs=out_specs,
        out_shape=out_shape,
        scratch_shapes=[pltpu.VMEM((tm, D_MODEL), BF16), pltpu.VMEM((tm, D_MODEL), F32)],
        input_output_aliases={k: k for k in range(len(prev))},
        compiler_params=_cparams(("arbitrary", "arbitrary"), FFN_VMEM_LIMIT),
        name=name,
    )(*prev, x, pre, post, nxt, wg, wu, wd, *[w for w, _, _ in casts])


WIDE_TN = 1024


def _dot_nt(a, b_t):
    return lax.dot_general(a, b_t, (((1,), (1,)), ((), ())), preferred_element_type=F32)


def _mm_plain_kernel(h_ref, w_ref, o_ref):
    o_ref[...] = _dot_nt(h_ref[...], w_ref[...]).astype(o_ref.dtype)


def _z_dt_kernel(h_ref, wz_ref, wdt_ref, b_ref, z_ref, dt_ref):
    h = h_ref[...]
    z_ref[...] = _dot_nt(h, wz_ref[...])

    @pl.when(pl.program_id(1) == 0)
    def _():
        dt = jax.nn.softplus(_dot_nt(h, wdt_ref[...]) + b_ref[...])
        lane = lax.broadcasted_iota(jnp.int32, dt.shape, 1)
        dt_ref[...] = jnp.where(lane < HEADS, dt, 0.0)


def _proj_z_dt(h, wt, dt_row0, dt_bias, *, tm=1024, tn=WIDE_TN):
    m, k = h.shape
    assert dt_row0 % LANES == 0
    return pl.pallas_call(
        _z_dt_kernel,
        grid=(m // tm, DINNER // tn),
        in_specs=[
            pl.BlockSpec((tm, k), lambda i, j: (i, 0)),
            pl.BlockSpec((tn, k), lambda i, j: (j, 0)),
            pl.BlockSpec((LANES, k), lambda i, j: (dt_row0 // LANES, 0)),
            pl.BlockSpec((1, LANES), lambda i, j: (0, 0)),
        ],
        out_specs=[
            pl.BlockSpec((tm, tn), lambda i, j: (i, j)),
            pl.BlockSpec((tm, LANES), lambda i, j: (i, 0)),
        ],
        out_shape=[jax.ShapeDtypeStruct((m, DINNER), F32), jax.ShapeDtypeStruct((m, LANES), F32)],
        compiler_params=_cparams(("parallel", "arbitrary")),
        name="proj_z_dt",
    )(h, wt, wt, dt_bias)


def _mm_glu_kernel(h_ref, wa_ref, wb_ref, o_ref):
    h = h_ref[...]
    o_ref[...] = _dot_nt(h, wa_ref[...]) * jax.nn.sigmoid(_dot_nt(h, wb_ref[...]))


def _mm(kernel, h, ws, n_out, out_dtype, name, *, tm=1024, tn=512):
    m, k = h.shape
    tn = min(tn, n_out)
    grid = (m // tm, n_out // tn)
    in_specs = [pl.BlockSpec((tm, k), lambda i, j: (i, 0))]
    for _, r0 in ws:
        assert r0 % tn == 0
        in_specs.append(pl.BlockSpec((tn, k), functools.partial(lambda i, j, o: (o + j, 0), o=r0 // tn)))
    return pl.pallas_call(
        kernel,
        grid=grid,
        in_specs=in_specs,
        out_specs=pl.BlockSpec((tm, tn), lambda i, j: (i, j)),
        out_shape=jax.ShapeDtypeStruct((m, n_out), out_dtype),
        compiler_params=_cparams(("parallel", "arbitrary")),
        name=name,
    )(h, *[w for w, _ in ws])


def _block_masks(seg_len):
    t = lax.broadcasted_iota(jnp.int32, (CHUNK, CHUNK), 0)
    s = lax.broadcasted_iota(jnp.int32, (CHUNK, CHUNK), 1)
    if seg_len == CHUNK:
        same = t >= 0
    else:
        sh = seg_len.bit_length() - 1
        same = (t >> sh) == (s >> sh)
    causal = jnp.logical_and(same, s <= t)
    return same, causal


CONV_STRIP = 512


def _conv4_silu(full, w8_ref, cb_ref, cols):
    n = full.shape[-2] - SUBLANES
    width = full.shape[-1]
    acc = None
    for k in range(SSM_K):
        win = full[..., 5 + k:5 + k + n, :].reshape(CHUNK // SUBLANES, SUBLANES, width)
        t = w8_ref[k, :, cols][None] * win
        acc = t if acc is None else acc + t
    return jax.nn.silu(acc.reshape(CHUNK, width) + cb_ref[:, cols])


def _conv4_strips(load_strip, w8_ref, cb_ref):
    strips = []
    for c0 in range(0, CONV_DIM, CONV_STRIP):
        cols = slice(c0, c0 + CONV_STRIP)
        strips.append(_conv4_silu(load_strip(cols), w8_ref, cb_ref, cols))
    return strips


def _ssd_intra(strips, dt, alog_row, e01, seg_len):
    same, causal = _block_masks(seg_len)
    n_x = DINNER // CONV_STRIP
    xs = jnp.concatenate(strips[:n_x], axis=1)
    bm = strips[n_x]
    cm = strips[n_x + 1]
    a_row = -jnp.exp(alog_row)
    dta = dt * a_row
    causal01 = jnp.where(causal, 1.0, 0.0).astype(BF16)
    same01 = jnp.where(same, 1.0, 0.0).astype(BF16)
    cum = _dot01_lhs(causal01, dta)
    cum_end = _dot01_lhs(same01, dta)
    cum_t = cum.T
    dt_t = dt.T
    ecum_x = _dot01_rhs(jnp.exp(cum), e01)
    coef_x = _dot01_rhs(jnp.exp(cum_end - cum) * dt, e01)

    lane = lax.broadcasted_iota(jnp.int32, (CHUNK, LANES), 1)
    first_head = lane < HEADDIM
    ys = []
    for g in range(GROUPS):
        cg = cm[:, g * STATE:(g + 1) * STATE].astype(BF16)
        bg = bm[:, g * STATE:(g + 1) * STATE].astype(BF16)
        cb = lax.dot_general(cg, bg, (((1,), (1,)), ((), ())), preferred_element_type=F32)
        for p in range(HPG // 2):
            ws = []
            for e in (g * HPG + 2 * p, g * HPG + 2 * p + 1):
                seg = cum[:, e:e + 1] - cum_t[e:e + 1, :]
                dec = jnp.exp(jnp.where(causal, seg, -jnp.inf))
                ws.append(((cb * dec) * dt_t[e:e + 1, :]).astype(BF16))
            w2 = jnp.concatenate(ws, axis=1)
            col = (g * HPG + 2 * p) * HEADDIM
            x2 = xs[:, col:col + LANES]
            r2 = jnp.concatenate([jnp.where(first_head, x2, 0.0),
                                  jnp.where(first_head, 0.0, x2)], axis=0).astype(BF16)
            ys.append(_dot(w2, r2))
    y_diag = jnp.concatenate(ys, axis=1)
    return xs, cm, bm, y_diag, coef_x, ecum_x


def _store_gated_norm(yn_ref, y, z_ref, nrm_ref):
    for r in range(0, CHUNK, BF16_ROWS):
        rows = slice(r, r + BF16_ROWS)
        yn_ref[rows, :] = _rms(y[rows, :] * jax.nn.silu(z_ref[rows, :]), nrm_ref[...]).astype(BF16)


def _ssd_prompt_kernel(xbc_ref, dt_ref, z_ref, w8_ref, cb_ref, e_ref, alog_ref, dx_ref, nrm_ref,
                       yn_ref, hout_ref, xp_ref, ht_ref):
    c = pl.program_id(1)

    @pl.when(c == 0)
    def _():
        xp_ref[0:SUBLANES, :] = jnp.zeros((SUBLANES, CONV_DIM), F32)
        ht_ref[...] = jnp.zeros_like(ht_ref)

    xp_ref[SUBLANES:SUBLANES + CHUNK, :] = xbc_ref[...]
    strips = _conv4_strips(lambda cols: xp_ref[:, cols], w8_ref, cb_ref)
    xp_ref[0:SUBLANES, :] = xbc_ref[CHUNK - SUBLANES:CHUNK, :]

    xs, cm, bm, y_diag, coef_x, ecum_x = _ssd_intra(strips, dt_ref[...], alog_ref[...], e_ref[...], CHUNK)
    xcoef = (xs * coef_x).astype(BF16)
    dec_row = ecum_x[CHUNK - 1:CHUNK, :]
    y_off = []
    for g in range(GROUPS):
        gs = slice(g * GW, (g + 1) * GW)
        ht_g = ht_ref[:, gs]
        cg = cm[:, g * STATE:(g + 1) * STATE].astype(BF16)
        y_off.append(_dot(cg, ht_g.astype(BF16)))
        bg_t = bm[:, g * STATE:(g + 1) * STATE].T.astype(BF16)
        ht_ref[:, gs] = ht_g * dec_row[:, gs] + _dot(bg_t, xcoef[:, gs])
    y = y_diag + jnp.concatenate(y_off, axis=1) * ecum_x + dx_ref[...] * xs
    _store_gated_norm(yn_ref, y, z_ref, nrm_ref)

    @pl.when(c == pl.num_programs(1) - 1)
    def _():
        hout_ref[...] = ht_ref[...].T


def _ssm_param_specs(const):
    return [
        pl.BlockSpec((SSM_K, SUBLANES, CONV_DIM), lambda *_: (0, 0, 0)),
        pl.BlockSpec((1, CONV_DIM), const),
        pl.BlockSpec((3 * LANES, DINNER), const),
        pl.BlockSpec((1, LANES), const),
        pl.BlockSpec((1, DINNER), const),
        pl.BlockSpec((1, DINNER), const),
    ]


def _ssd_prompt(xbc, dt, z, *params):
    nc = SEQ // CHUNK
    blk = lambda b, c: (b * nc + c, 0)
    const = lambda b, c: (0, 0)
    return pl.pallas_call(
        _ssd_prompt_kernel,
        grid=(BATCH, nc),
        in_specs=[
            pl.BlockSpec((CHUNK, CONV_DIM), blk),
            pl.BlockSpec((CHUNK, LANES), blk),
            pl.BlockSpec((CHUNK, DINNER), blk),
        ] + _ssm_param_specs(const),
        out_specs=[
            pl.BlockSpec((CHUNK, DINNER), blk),
            pl.BlockSpec((None, DINNER, STATE), lambda b, c: (b, 0, 0)),
        ],
        out_shape=[
            jax.ShapeDtypeStruct((N_TOK, DINNER), BF16),
            jax.ShapeDtypeStruct((BATCH, DINNER, STATE), F32),
        ],
        scratch_shapes=[pltpu.VMEM((CHUNK + SUBLANES, CONV_DIM), F32), pltpu.VMEM((STATE, DINNER), F32)],
        compiler_params=_cparams(("parallel", "arbitrary")),
        name="ssd_prompt",
    )(xbc, dt, z, *params)


def _ssd_sample_kernel(yn_any, xbc_ref, cst_ref, dt_ref, z_ref, h0_ref, w8_ref, cb_ref, e_ref,
                       alog_ref, dx_ref, nrm_ref, yn_ref, hout_ref,
                       xp_ref, cm_ref, bt_ref, xcoef_ref, ecx_ref, y_ref):
    del yn_any
    j = pl.program_id(1)

    @pl.when(j == 0)
    def _():
        xp_ref[:, SUBLANES - (SSM_K - 1):SUBLANES, :] = cst_ref[...]
        xp_ref[:, SUBLANES:SUBLANES + DEC_SEQ, :] = xbc_ref[...]
        strips = _conv4_strips(lambda cols: xp_ref[:, :, cols], w8_ref, cb_ref)
        xs, cm, bm, y_diag, coef_x, ecum_x = _ssd_intra(
            strips, dt_ref[...], alog_ref[...], e_ref[...], DEC_SEQ)
        cm_ref[...] = cm
        for g in range(GROUPS):
            bt_ref[g] = bm[:, g * STATE:(g + 1) * STATE].T
        xcoef_ref[...] = (xs * coef_x).astype(BF16)
        ecx_ref[...] = ecum_x
        y_ref[...] = y_diag + dx_ref[...] * xs

    col = lax.broadcasted_iota(jnp.int32, (STATE, CHUNK), 1)
    base = j * (SEQ_PER_STEP * DEC_SEQ)
    for p in range(SEQ_PER_STEP):
        r0 = pl.multiple_of(base + p * DEC_SEQ, DEC_SEQ)
        rows = pl.ds(r0, DEC_SEQ)
        ht = h0_ref[p].T
        ht_b = ht.astype(BF16)
        ecx = ecx_ref[rows, :]
        cj = cm_ref[rows, :].astype(BF16)
        mine = jnp.logical_and(col >= r0, col < r0 + DEC_SEQ)
        y_off, new = [], []
        for g in range(GROUPS):
            gs = slice(g * GW, (g + 1) * GW)
            y_off.append(_dot(cj[:, g * STATE:(g + 1) * STATE], ht_b[:, gs]))
            bg_t = jnp.where(mine, bt_ref[g], 0.0).astype(BF16)
            new.append(_dot(bg_t, xcoef_ref[:, gs]))
        y_ref[rows, :] += jnp.concatenate(y_off, axis=1) * ecx
        ht_new = ht * ecx[DEC_SEQ - 1:DEC_SEQ, :] + jnp.concatenate(new, axis=1)
        hout_ref[p] = ht_new.T

    @pl.when(j == pl.num_programs(1) - 1)
    def _():
        _store_gated_norm(yn_ref, y_ref, z_ref, nrm_ref)


def _ssd_sample(yn_all, xbc3, cst, dt, z, h0, *params):
    nb = DEC_BATCH // SEGS
    off = N_PROMPT // CHUNK
    const = lambda i, j: (0, 0)
    tok = lambda i, j: (off + i, 0)
    seq = lambda i, j: (i * (SEGS // SEQ_PER_STEP) + j, 0, 0)
    return pl.pallas_call(
        _ssd_sample_kernel,
        grid=(nb, SEGS // SEQ_PER_STEP),
        in_specs=[
            pl.BlockSpec(memory_space=pl.ANY),
            pl.BlockSpec((SEGS, DEC_SEQ, CONV_DIM), lambda i, j: (off + i, 0, 0)),
            pl.BlockSpec((SEGS, SSM_K - 1, CONV_DIM), lambda i, j: (i, 0, 0)),
            pl.BlockSpec((CHUNK, LANES), tok),
            pl.BlockSpec((CHUNK, DINNER), tok),
            pl.BlockSpec((SEQ_PER_STEP, DINNER, STATE), seq),
        ] + _ssm_param_specs(const),
        out_specs=[
            pl.BlockSpec((CHUNK, DINNER), tok),
            pl.BlockSpec((SEQ_PER_STEP, DINNER, STATE), seq),
        ],
        out_shape=[
            jax.ShapeDtypeStruct((N_TOK, DINNER), BF16),
            jax.ShapeDtypeStruct((DEC_BATCH, DINNER, STATE), F32),
        ],
        scratch_shapes=[
            pltpu.VMEM((SEGS, SUBLANES + DEC_SEQ, CONV_DIM), F32),
            pltpu.VMEM((CHUNK, GROUPS * STATE), F32),
            pltpu.VMEM((GROUPS, STATE, CHUNK), F32),
            pltpu.VMEM((CHUNK, DINNER), BF16),
            pltpu.VMEM((CHUNK, DINNER), F32),
            pltpu.VMEM((CHUNK, DINNER), F32),
        ],
        input_output_aliases={0: 0},
        compiler_params=_cparams(("parallel", "arbitrary")),
        name="ssd_sample",
    )(yn_all, xbc3, cst, dt, z, h0, *params)


CC_PAD = 32
CC_FIRST = CC_PAD - (CC_K - 1)
CC_LN_ROWS = 32
CC_STRIP = 512


def _cc_kernel(*refs, n_seg, seg_len, carry, aliased):
    refs = list(refs)
    if aliased:
        refs.pop(0)
    u_ref = refs.pop(0)
    st_ref = None if carry else refs.pop(0)
    w8_ref, b_ref, g_ref, be_ref, o_ref, xp_ref, res_ref = refs
    if carry:
        @pl.when(pl.program_id(1) == 0)
        def _():
            xp_ref[:, 0:CC_PAD, :] = jnp.zeros((n_seg, CC_PAD, CC_DIM), F32)
    else:
        xp_ref[:, CC_FIRST:CC_PAD, :] = st_ref[...]
    xp_ref[:, CC_PAD:CC_PAD + seg_len, :] = u_ref[...]

    rb = min(seg_len, 64)
    n_shift = rb + CC_PAD - SUBLANES
    for s in range(n_seg):
        for t0 in range(0, seg_len, rb):
            for c0 in range(0, CC_DIM, CC_STRIP):
                cols = slice(c0, c0 + CC_STRIP)
                full = xp_ref[s, t0:t0 + rb + CC_PAD, cols]
                acc = None
                for b in range(SUBLANES):
                    xb = full if b == 0 else full[b:b + n_shift, :]
                    for a in range(CC_PAD // SUBLANES + 1):
                        k = SUBLANES * a + b - CC_FIRST
                        if k < 0 or k >= CC_K:
                            continue
                        win = xb[SUBLANES * a:SUBLANES * a + rb, :].reshape(rb // SUBLANES, SUBLANES, CC_STRIP)
                        t = w8_ref[k, :, cols][None] * win
                        acc = t if acc is None else acc + t
                row = s * seg_len + t0
                res_ref[row:row + rb, cols] = acc.reshape(rb, CC_STRIP) + b_ref[:, cols]

    for r0 in range(0, n_seg * seg_len, CC_LN_ROWS):
        v = res_ref[r0:r0 + CC_LN_ROWS, :]
        mu = jnp.mean(v, axis=-1, keepdims=True)
        d = v - mu
        var = jnp.mean(d * d, axis=-1, keepdims=True)
        yv = (d * lax.rsqrt(var + EPS)) * g_ref[...] + be_ref[...]
        o_ref[r0:r0 + CC_LN_ROWS, :] = jax.nn.silu(yv).astype(BF16)
    if carry:
        xp_ref[:, 0:CC_PAD, :] = xp_ref[:, seg_len:seg_len + CC_PAD, :]


def _cc_param_specs(const):
    return [
        pl.BlockSpec((CC_K, SUBLANES, CC_DIM), lambda *_: (0, 0, 0)),
        pl.BlockSpec((1, CC_DIM), const),
        pl.BlockSpec((1, CC_DIM), const),
        pl.BlockSpec((1, CC_DIM), const),
    ]


def _cc_prompt(u3, *params, tq):
    nc = SEQ // tq
    const = lambda i, c: (0, 0)
    return pl.pallas_call(
        functools.partial(_cc_kernel, n_seg=1, seg_len=tq, carry=True, aliased=False),
        grid=(BATCH, nc),
        in_specs=[pl.BlockSpec((1, tq, CC_DIM), lambda i, c: (i * nc + c, 0, 0))] + _cc_param_specs(const),
        out_specs=pl.BlockSpec((tq, CC_DIM), lambda i, c: (i * nc + c, 0)),
        out_shape=jax.ShapeDtypeStruct((N_TOK, CC_DIM), BF16),
        scratch_shapes=[pltpu.VMEM((1, CC_PAD + tq, CC_DIM), F32), pltpu.VMEM((tq, CC_DIM), F32)],
        compiler_params=_cparams(("parallel", "arbitrary")),
        name="cc_prompt",
    )(u3, *params)


def _cc_sample(uc_all, u3, st, *params, n_seg):
    const = lambda i: (0, 0)
    rows = n_seg * DEC_SEQ
    off = N_PROMPT // rows
    return pl.pallas_call(
        functools.partial(_cc_kernel, n_seg=n_seg, seg_len=DEC_SEQ, carry=False, aliased=True),
        grid=(DEC_BATCH // n_seg,),
        in_specs=[
            pl.BlockSpec(memory_space=pl.ANY),
            pl.BlockSpec((n_seg, DEC_SEQ, CC_DIM), lambda i: (off + i, 0, 0)),
            pl.BlockSpec((n_seg, CC_K - 1, CC_DIM), lambda i: (i, 0, 0)),
        ] + _cc_param_specs(const),
        out_specs=pl.BlockSpec((rows, CC_DIM), lambda i: (off + i, 0)),
        out_shape=jax.ShapeDtypeStruct((N_TOK, CC_DIM), BF16),
        scratch_shapes=[pltpu.VMEM((n_seg, CC_PAD + DEC_SEQ, CC_DIM), F32),
                        pltpu.VMEM((rows, CC_DIM), F32)],
        input_output_aliases={0: 0},
        compiler_params=_cparams(("parallel",)),
        name="cc_sample",
    )(uc_all, u3, st, *params)


def _merge_kernel(h_ref, yn_ref, uc_ref, wgs_ref, wgc_ref, wso_ref, wco_ref, o_ref):
    h = h_ref[...]
    g_ssd = jax.nn.sigmoid(_dot_nt(h, wgs_ref[...]))
    g_cc = jax.nn.sigmoid(_dot_nt(h, wgc_ref[...]))
    ssd_out = _dot(yn_ref[...], wso_ref[...])
    cc_out = _dot(uc_ref[...], wco_ref[...])
    o_ref[...] = (g_ssd * ssd_out + g_cc * cc_out).astype(BF16)


def _merge(h, yn, uc, wgate_t, wso, wco, *, tm=1024, tn=512):
    m = h.shape[0]
    row = lambda i, j: (i, 0)
    colw = lambda i, j: (0, j)
    roww = lambda i, j: (j, 0)
    roww2 = lambda i, j: (D_MODEL // tn + j, 0)
    return pl.pallas_call(
        _merge_kernel,
        grid=(m // tm, D_MODEL // tn),
        in_specs=[
            pl.BlockSpec((tm, D_MODEL), row),
            pl.BlockSpec((tm, DINNER), row),
            pl.BlockSpec((tm, CC_DIM), row),
            pl.BlockSpec((tn, D_MODEL), roww),
            pl.BlockSpec((tn, D_MODEL), roww2),
            pl.BlockSpec((DINNER, tn), colw),
            pl.BlockSpec((CC_DIM, tn), colw),
        ],
        out_specs=pl.BlockSpec((tm, tn), lambda i, j: (i, j)),
        out_shape=jax.ShapeDtypeStruct((m, D_MODEL), BF16),
        compiler_params=_cparams(("parallel", "arbitrary")),
        name="merge",
    )(h, yn, uc, wgate_t, wgate_t, wso, wco)


def _oproj_kernel(m_ref, x_ref, w_ref, nrm_ref, o_ref):
    o_ref[...] = x_ref[...] + _rms(_dot(m_ref[...], w_ref[...]), nrm_ref[...])


def _oproj(mg, x, w, nrm, *, tm=512):
    m = x.shape[0]
    row = lambda i: (i, 0)
    const = lambda i: (0, 0)
    return pl.pallas_call(
        _oproj_kernel,
        grid=(m // tm,),
        in_specs=[
            pl.BlockSpec((tm, D_MODEL), row),
            pl.BlockSpec((tm, D_MODEL), row),
            pl.BlockSpec((D_MODEL, D_MODEL), const),
            pl.BlockSpec((1, D_MODEL), const),
        ],
        out_specs=pl.BlockSpec((tm, D_MODEL), row),
        out_shape=jax.ShapeDtypeStruct((m, D_MODEL), F32),
        compiler_params=_cparams(("parallel",)),
        name="oproj",
    )(mg, x, w, nrm)


def _row(v):
    return v.reshape(1, -1).astype(F32)


def _pad_lanes(v):
    return jnp.pad(v.astype(F32), (0, LANES - v.shape[0])).reshape(1, LANES)


def _taps_on_sublanes(w):
    return jnp.broadcast_to(w.astype(F32)[:, None, :], (w.shape[0], SUBLANES, w.shape[1]))


def kernel(x_prompt, x_sample, state_ssm, state_ssm_conv, state_cc_conv, ffn1_pre_norm, ffn1_post_norm, ffn1_w_gate, ffn1_w_up, ffn1_w_down, mix_pre_norm, mix_post_norm, w_in, ssm_conv_w, ssm_conv_b, ssm_dt_bias, ssm_A_log, ssm_D, ssm_norm, w_ssd_out, cc_conv_w, cc_conv_b, cc_ln_g, cc_ln_b, w_cc_out, w_o, ffn2_pre_norm, ffn2_post_norm, ffn2_w_gate, ffn2_w_up, ffn2_w_down):
    l = 0
    npt, nst = N_PROMPT // FFN_TM, N_SAMPLE // FFN_TM

    f1 = (_row(ffn1_pre_norm[l]), _row(ffn1_post_norm[l]), _row(mix_pre_norm[l]))
    x1, h, w1g, w1u, w1d = _ffn(
        x_sample.reshape(N_SAMPLE, D_MODEL), 0, nst, *f1, ffn1_w_gate[l], ffn1_w_up[l], ffn1_w_down[l],
        out_rows=N_TOK, out_tile0=npt, emit_h=True, name="ffn1_sample", tf=FFN_TF // 2)
    w_in_t = w_in[l].T
    c_xbc = DINNER
    c_dt = c_xbc + CONV_DIM
    c_glu = c_dt + HEADS
    c_gate = c_glu + 2 * CC_DIM
    whole = lambda w: (w, 0, w.shape[0])
    later = (whole(ffn2_w_gate[l]), whole(ffn2_w_up[l]), whole(ffn2_w_down[l]),
             (w_in_t, 0, c_dt + LANES), (w_in_t, c_glu, 2 * CC_DIM), (w_in_t, c_gate, 2 * D_MODEL),
             whole(w_ssd_out[l]), whole(w_cc_out[l]), whole(w_o[l]))
    x1, h, w2g, w2u, w2d, wt, wt_glu, wt_gate, w_so, w_co, w_ob = _ffn(
        x_prompt.reshape(N_PROMPT, D_MODEL), 0, npt, *f1, w1g, w1u, w1d, out_rows=N_TOK, out_tile0=0,
        emit_h=True, prev=(x1, h), casts=later, name="ffn1_prompt")

    e01 = (jnp.arange(LANES)[:, None] == (jnp.arange(DINNER)[None, :] // HEADDIM)).astype(BF16)
    e01 = jnp.concatenate([e01] * 3, axis=0)
    d_x = jnp.repeat(ssm_D[l].astype(F32), HEADDIM).reshape(1, DINNER)

    z, dt = _proj_z_dt(h, wt, c_dt, _pad_lanes(ssm_dt_bias[l]))
    xbc = _mm(_mm_plain_kernel, h, [(wt, c_xbc)], CONV_DIM, F32, "proj_xbc", tn=WIDE_TN)
    u = _mm(_mm_glu_kernel, h, [(wt_glu, 0), (wt_glu, CC_DIM)], CC_DIM, F32, "proj_glu")

    ssm_params = (_taps_on_sublanes(ssm_conv_w[l]), _row(ssm_conv_b[l]), e01, _pad_lanes(ssm_A_log[l]),
                  d_x, _row(ssm_norm[l]))
    yn, ssm_p = _ssd_prompt(xbc, dt, z, *ssm_params)
    yn, ssm_s = _ssd_sample(yn, xbc.reshape(N_TOK // DEC_SEQ, DEC_SEQ, CONV_DIM), state_ssm_conv[l], dt, z,
                            state_ssm[l].reshape(DEC_BATCH, DINNER, STATE), *ssm_params)

    cc_params = (_taps_on_sublanes(cc_conv_w[l]), _row(cc_conv_b[l]), _row(cc_ln_g[l]), _row(cc_ln_b[l]))
    tq = 256
    uc = _cc_prompt(u.reshape(N_TOK // tq, tq, CC_DIM), *cc_params, tq=tq)
    uc = _cc_sample(uc, u.reshape(N_TOK // DEC_SEQ, DEC_SEQ, CC_DIM), state_cc_conv[l], *cc_params, n_seg=16)

    mg = _merge(h, yn, uc, wt_gate, w_so, w_co)
    x2 = _oproj(mg, x1, w_ob, _row(mix_post_norm[l]))

    f2 = (_row(ffn2_pre_norm[l]), _row(ffn2_post_norm[l]), _row(ffn2_post_norm[l]), w2g, w2u, w2d)
    (y_p,) = _ffn(x2, 0, npt, *f2, out_rows=N_PROMPT, out_tile0=0, emit_h=False, name="ffn2_prompt")
    (y_s,) = _ffn(x2, npt, nst, *f2, out_rows=N_SAMPLE, out_tile0=0, emit_h=False, name="ffn2_sample")

    y_prompt = y_p.reshape(BATCH, SEQ, D_MODEL)
    y_sample = y_s.reshape(DEC_BATCH, DEC_SEQ, D_MODEL)
    new_ssm_p = ssm_p.reshape(1, BATCH, HEADS, HEADDIM, STATE)
    new_ssm_s = ssm_s.reshape(1, DEC_BATCH, HEADS, HEADDIM, STATE)
    tail = lambda a, n: jnp.stack([a[(b + 1) * SEQ - n:(b + 1) * SEQ] for b in range(BATCH)])[None]
    new_sconv_p = tail(xbc, SSM_K - 1)
    new_cc_p = tail(u, CC_K - 1)
    xbc_s3 = xbc[N_PROMPT:].reshape(DEC_BATCH, DEC_SEQ, CONV_DIM)
    new_sconv_s = xbc_s3[:, DEC_SEQ - (SSM_K - 1):][None]
    u_s3 = u[N_PROMPT:].reshape(DEC_BATCH, DEC_SEQ, CC_DIM)
    new_cc_s = jnp.concatenate([state_cc_conv[l][:, DEC_SEQ:], u_s3], axis=1)[None]
    return (y_prompt, y_sample, new_ssm_p, new_sconv_p, new_cc_p, new_ssm_s, new_sconv_s, new_cc_s)
```

```python
import functools

import jax
import jax.numpy as jnp
from jax import lax
from jax.experimental import pallas as pl
from jax.experimental.pallas import tpu as pltpu

F32 = jnp.float32
BF16 = jnp.bfloat16

D_MODEL = 2048
D_FF = 5632
BATCH, SEQ = 4, 2048
DEC_BATCH, DEC_SEQ = 128, 8
N_PROMPT = BATCH * SEQ
N_SAMPLE = DEC_BATCH * DEC_SEQ
N_TOK = N_PROMPT + N_SAMPLE
HEADS, HEADDIM, GROUPS, STATE = 32, 64, 4, 128
HPG = HEADS // GROUPS
DINNER = HEADS * HEADDIM
GW = HPG * HEADDIM
SSM_K = 4
CONV_DIM = DINNER + 2 * GROUPS * STATE
CC_DIM = D_MODEL // 2
CC_K = 31
EPS = 1e-6

LANES = 128
SUBLANES = 8
CHUNK = 128
SEGS = CHUNK // DEC_SEQ
SEQ_PER_STEP = 4
VMEM_LIMIT = 48 * 1024 * 1024
FFN_VMEM_LIMIT = 58 * 1024 * 1024


def _cparams(sem, vmem=VMEM_LIMIT):
    return pltpu.CompilerParams(dimension_semantics=sem, vmem_limit_bytes=vmem)


def _rms(x, g):
    ms = jnp.mean(x * x, axis=-1, keepdims=True)
    return (x * lax.rsqrt(ms + EPS)) * g


def _dot(a, b):
    return jnp.dot(a, b, preferred_element_type=F32)


def _split3(x):
    hi = x.astype(BF16)
    r = x - hi.astype(F32)
    mid = r.astype(BF16)
    lo = (r - mid.astype(F32)).astype(BF16)
    return hi, mid, lo


def _dot01_rhs(x, m01x3):
    return _dot(jnp.concatenate(_split3(x), axis=1), m01x3)


def _dot01_lhs(m01, x):
    return _dot(jnp.concatenate([m01] * 3, axis=1), jnp.concatenate(_split3(x), axis=0))


BF16_ROWS = 16
NORM_ROWS = 8
FFN_TM = 512
FFN_TF = 512


def _ffn_kernel(*refs, n_prev, n_cast, emit_h, f32_weights):
    refs = refs[n_prev:]
    x_ref, pre_ref, post_ref, nxt_ref, wg_ref, wu_ref, wd_ref = refs[:7]
    src_refs = refs[7:7 + n_cast]
    outs = refs[7 + n_cast:]
    y_ref = outs[0]
    ho_ref = outs[1] if emit_h else None
    dst_refs = outs[1 + emit_h:1 + emit_h + n_cast]
    own_refs = outs[1 + emit_h + n_cast:-2]
    h_scr, acc_scr = outs[-2:]
    j = pl.program_id(1)

    @pl.when(j == 0)
    def _():
        h_scr[...] = _rms(x_ref[...], pre_ref[...]).astype(BF16)
        acc_scr[...] = jnp.zeros_like(acc_scr)

    _cast_side_job(src_refs, dst_refs)

    wg, wu, wd = wg_ref[...], wu_ref[...], wd_ref[...]
    if f32_weights:
        wg, wu, wd = wg.astype(BF16), wu.astype(BF16), wd.astype(BF16)
        for own_ref, w in zip(own_refs, (wg, wu, wd)):
            own_ref[...] = w
    h = h_scr[...]
    g = _dot(h, wg)
    u = _dot(h, wu)
    a = (jax.nn.silu(g) * u).astype(BF16)
    acc_scr[...] += _dot(a, wd)

    @pl.when(j == pl.num_programs(1) - 1)
    def _():
        if emit_h:
            xn = x_ref[...] + 0.5 * _rms(acc_scr[...], post_ref[...])
            y_ref[...] = xn
            ho_ref[...] = _rms(xn, nxt_ref[...]).astype(BF16)
        else:
            for r in range(0, x_ref.shape[0], NORM_ROWS):
                rows = slice(r, r + NORM_ROWS)
                y_ref[rows, :] = x_ref[rows, :] + 0.5 * _rms(acc_scr[rows, :], post_ref[...])


def _cast_chunk_rows(rows, n_steps):
    return BF16_ROWS * pl.cdiv(pl.cdiv(rows, BF16_ROWS), n_steps)


def _cast_streams(casts, n_outer, n_inner):
    in_specs, out_specs, out_shape = [], [], []
    for w, row0, rows in casts:
        cols = w.shape[1]
        rc = _cast_chunk_rows(rows, n_outer * n_inner)
        assert row0 % rc == 0
        chunk = lambda i, j, first, last: (first + jnp.minimum(i * n_inner + j, last), 0)
        last = pl.cdiv(rows, rc) - 1
        in_specs.append(pl.BlockSpec((rc, cols), functools.partial(chunk, first=row0 // rc, last=last)))
        out_specs.append(pl.BlockSpec((rc, cols), functools.partial(chunk, first=0, last=last)))
        out_shape.append(jax.ShapeDtypeStruct((rows, cols), BF16))
    return in_specs, out_specs, out_shape


def _cast_side_job(src_refs, dst_refs):
    for src_ref, dst_ref in zip(src_refs, dst_refs):
        dst_ref[...] = src_ref[...].astype(BF16)


def _ffn(x, x_tile0, n_tiles, pre, post, nxt, wg, wu, wd, *, out_rows, out_tile0, emit_h,
         prev=(), casts=(), name, tm=FFN_TM, tf=FFN_TF):
    nj = D_FF // tf
    vec = pl.BlockSpec((1, D_MODEL), lambda i, j: (0, 0))
    mode = dict(pipeline_mode=pl.Buffered(1)) if n_tiles == 1 else {}
    tile = lambda t0: pl.BlockSpec((tm, D_MODEL), functools.partial(lambda i, j, t: (t + i, 0), t=t0), **mode)
    out_shape = [jax.ShapeDtypeStruct((out_rows, D_MODEL), F32)]
    out_specs = [tile(out_tile0)]
    if emit_h:
        out_shape.append(jax.ShapeDtypeStruct((out_rows, D_MODEL), BF16))
        out_specs.append(tile(out_tile0))
    cast_specs, cast_out_specs, cast_out_shape = _cast_streams(casts, n_tiles, nj)
    out_specs += cast_out_specs
    out_shape += cast_out_shape
    w_specs = [
        pl.BlockSpec((D_MODEL, tf), lambda i, j: (0, j)),
        pl.BlockSpec((D_MODEL, tf), lambda i, j: (0, j)),
        pl.BlockSpec((tf, D_MODEL), lambda i, j: (j, 0)),
    ]
    f32_weights = wg.dtype == F32
    if f32_weights:
        once = lambda i, j: jnp.where(i == 0, j, nj - 1)
        out_shape += [jax.ShapeDtypeStruct(w.shape, BF16) for w in (wg, wu, wd)]
        out_specs += [
            pl.BlockSpec((D_MODEL, tf), lambda i, j: (0, once(i, j))),
            pl.BlockSpec((D_MODEL, tf), lambda i, j: (0, once(i, j))),
            pl.BlockSpec((tf, D_MODEL), lambda i, j: (once(i, j), 0)),
        ]
    return pl.pallas_call(
        functools.partial(_ffn_kernel, n_prev=len(prev), n_cast=len(casts), emit_h=emit_h,
                          f32_weights=f32_weights),
        grid=(n_tiles, nj),
        in_specs=[pl.BlockSpec(memory_space=pl.ANY)] * len(prev)
        + [tile(x_tile0), vec, vec, vec] + w_specs + cast_specs,
        out_specs=out_specs,
        out_shape=out_shape,
        scratch_shapes=[pltpu.VMEM((tm, D_MODEL), BF16), pltpu.VMEM((tm, D_MODEL), F32)],
        input_output_aliases={k: k for k in range(len(prev))},
        compiler_params=_cparams(("arbitrary", "arbitrary"), FFN_VMEM_LIMIT),
        name=name,
    )(*prev, x, pre, post, nxt, wg, wu, wd, *[w for w, _, _ in casts])


WIDE_TN = 1024


def _dot_nt(a, b_t):
    return lax.dot_general(a, b_t, (((1,), (1,)), ((), ())), preferred_element_type=F32)


def _mm_plain_kernel(h_ref, w_ref, o_ref):
    o_ref[...] = _dot_nt(h_ref[...], w_ref[...]).astype(o_ref.dtype)


def _z_dt_kernel(h_ref, wz_ref, wdt_ref, b_ref, z_ref, dt_ref):
    h = h_ref[...]
    z_ref[...] = _dot_nt(h, wz_ref[...])

    @pl.when(pl.program_id(1) == 0)
    def _():
        dt = jax.nn.softplus(_dot_nt(h, wdt_ref[...]) + b_ref[...])
        lane = lax.broadcasted_iota(jnp.int32, dt.shape, 1)
        dt_ref[...] = jnp.where(lane < HEADS, dt, 0.0)


def _proj_z_dt(h, wt, dt_row0, dt_bias, *, tm=1024, tn=WIDE_TN):
    m, k = h.shape
    assert dt_row0 % LANES == 0
    return pl.pallas_call(
        _z_dt_kernel,
        grid=(m // tm, DINNER // tn),
        in_specs=[
            pl.BlockSpec((tm, k), lambda i, j: (i, 0)),
            pl.BlockSpec((tn, k), lambda i, j: (j, 0)),
            pl.BlockSpec((LANES, k), lambda i, j: (dt_row0 // LANES, 0)),
            pl.BlockSpec((1, LANES), lambda i, j: (0, 0)),
        ],
        out_specs=[
            pl.BlockSpec((tm, tn), lambda i, j: (i, j)),
            pl.BlockSpec((tm, LANES), lambda i, j: (i, 0)),
        ],
        out_shape=[jax.ShapeDtypeStruct((m, DINNER), F32), jax.ShapeDtypeStruct((m, LANES), F32)],
        compiler_params=_cparams(("parallel", "arbitrary")),
        name="proj_z_dt",
    )(h, wt, wt, dt_bias)


def _mm_glu_kernel(h_ref, wa_ref, wb_ref, o_ref):
    h = h_ref[...]
    o_ref[...] = _dot_nt(h, wa_ref[...]) * jax.nn.sigmoid(_dot_nt(h, wb_ref[...]))


def _mm(kernel, h, ws, n_out, out_dtype, name, *, tm=1024, tn=512):
    m, k = h.shape
    tn = min(tn, n_out)
    grid = (m // tm, n_out // tn)
    in_specs = [pl.BlockSpec((tm, k), lambda i, j: (i, 0))]
    for _, r0 in ws:
        assert r0 % tn == 0
        in_specs.append(pl.BlockSpec((tn, k), functools.partial(lambda i, j, o: (o + j, 0), o=r0 // tn)))
    return pl.pallas_call(
        kernel,
        grid=grid,
        in_specs=in_specs,
        out_specs=pl.BlockSpec((tm, tn), lambda i, j: (i, j)),
        out_shape=jax.ShapeDtypeStruct((m, n_out), out_dtype),
        compiler_params=_cparams(("parallel", "arbitrary")),
        name=name,
    )(h, *[w for w, _ in ws])


def _block_masks(seg_len):
    t = lax.broadcasted_iota(jnp.int32, (CHUNK, CHUNK), 0)
    s = lax.broadcasted_iota(jnp.int32, (CHUNK, CHUNK), 1)
    if seg_len == CHUNK:
        same = t >= 0
    else:
        sh = seg_len.bit_length() - 1
        same = (t >> sh) == (s >> sh)
    causal = jnp.logical_and(same, s <= t)
    return same, causal


CONV_STRIP = 512


def _conv4_silu(full, w8_ref, cb_ref, cols):
    n = full.shape[-2] - SUBLANES
    width = full.shape[-1]
    acc = None
    for k in range(SSM_K):
        win = full[..., 5 + k:5 + k + n, :].reshape(CHUNK // SUBLANES, SUBLANES, width)
        t = w8_ref[k, :, cols][None] * win
        acc = t if acc is None else acc + t
    return jax.nn.silu(acc.reshape(CHUNK, width) + cb_ref[:, cols])


def _conv4_strips(load_strip, w8_ref, cb_ref):
    strips = []
    for c0 in range(0, CONV_DIM, CONV_STRIP):
        cols = slice(c0, c0 + CONV_STRIP)
        strips.append(_conv4_silu(load_strip(cols), w8_ref, cb_ref, cols))
    return strips


def _ssd_intra(strips, dt, alog_row, e01, seg_len):
    same, causal = _block_masks(seg_len)
    n_x = DINNER // CONV_STRIP
    xs = jnp.concatenate(strips[:n_x], axis=1)
    bm = strips[n_x]
    cm = strips[n_x + 1]
    a_row = -jnp.exp(alog_row)
    dta = dt * a_row
    causal01 = jnp.where(causal, 1.0, 0.0).astype(BF16)
    same01 = jnp.where(same, 1.0, 0.0).astype(BF16)
    cum = _dot01_lhs(causal01, dta)
    cum_end = _dot01_lhs(same01, dta)
    cum_t = cum.T
    dt_t = dt.T
    ecum_x = _dot01_rhs(jnp.exp(cum), e01)
    coef_x = _dot01_rhs(jnp.exp(cum_end - cum) * dt, e01)

    lane = lax.broadcasted_iota(jnp.int32, (CHUNK, LANES), 1)
    first_head = lane < HEADDIM
    ys = []
    for g in range(GROUPS):
        cg = cm[:, g * STATE:(g + 1) * STATE].astype(BF16)
        bg = bm[:, g * STATE:(g + 1) * STATE].astype(BF16)
        cb = lax.dot_general(cg, bg, (((1,), (1,)), ((), ())), preferred_element_type=F32)
        for p in range(HPG // 2):
            ws = []
            for e in (g * HPG + 2 * p, g * HPG + 2 * p + 1):
                seg = cum[:, e:e + 1] - cum_t[e:e + 1, :]
                dec = jnp.exp(jnp.where(causal, seg, -jnp.inf))
                ws.append(((cb * dec) * dt_t[e:e + 1, :]).astype(BF16))
            w2 = jnp.concatenate(ws, axis=1)
            col = (g * HPG + 2 * p) * HEADDIM
            x2 = xs[:, col:col + LANES]
            r2 = jnp.concatenate([jnp.where(first_head, x2, 0.0),
                                  jnp.where(first_head, 0.0, x2)], axis=0).astype(BF16)
            ys.append(_dot(w2, r2))
    y_diag = jnp.concatenate(ys, axis=1)
    return xs, cm, bm, y_diag, coef_x, ecum_x


def _store_gated_norm(yn_ref, y, z_ref, nrm_ref):
    for r in range(0, CHUNK, BF16_ROWS):
        rows = slice(r, r + BF16_ROWS)
        yn_ref[rows, :] = _rms(y[rows, :] * jax.nn.silu(z_ref[rows, :]), nrm_ref[...]).astype(BF16)


def _ssd_prompt_kernel(xbc_ref, dt_ref, z_ref, w8_ref, cb_ref, e_ref, alog_ref, dx_ref, nrm_ref,
                       yn_ref, hout_ref, xp_ref, ht_ref):
    c = pl.program_id(1)

    @pl.when(c == 0)
    def _():
        xp_ref[0:SUBLANES, :] = jnp.zeros((SUBLANES, CONV_DIM), F32)
        ht_ref[...] = jnp.zeros_like(ht_ref)

    xp_ref[SUBLANES:SUBLANES + CHUNK, :] = xbc_ref[...]
    strips = _conv4_strips(lambda cols: xp_ref[:, cols], w8_ref, cb_ref)
    xp_ref[0:SUBLANES, :] = xbc_ref[CHUNK - SUBLANES:CHUNK, :]

    xs, cm, bm, y_diag, coef_x, ecum_x = _ssd_intra(strips, dt_ref[...], alog_ref[...], e_ref[...], CHUNK)
    xcoef = (xs * coef_x).astype(BF16)
    dec_row = ecum_x[CHUNK - 1:CHUNK, :]
    y_off = []
    for g in range(GROUPS):
        gs = slice(g * GW, (g + 1) * GW)
        ht_g = ht_ref[:, gs]
        cg = cm[:, g * STATE:(g + 1) * STATE].astype(BF16)
        y_off.append(_dot(cg, ht_g.astype(BF16)))
        bg_t = bm[:, g * STATE:(g + 1) * STATE].T.astype(BF16)
        ht_ref[:, gs] = ht_g * dec_row[:, gs] + _dot(bg_t, xcoef[:, gs])
    y = y_diag + jnp.concatenate(y_off, axis=1) * ecum_x + dx_ref[...] * xs
    _store_gated_norm(yn_ref, y, z_ref, nrm_ref)

    @pl.when(c == pl.num_programs(1) - 1)
    def _():
        hout_ref[...] = ht_ref[...].T


def _ssm_param_specs(const):
    return [
        pl.BlockSpec((SSM_K, SUBLANES, CONV_DIM), lambda *_: (0, 0, 0)),
        pl.BlockSpec((1, CONV_DIM), const),
        pl.BlockSpec((3 * LANES, DINNER), const),
        pl.BlockSpec((1, LANES), const),
        pl.BlockSpec((1, DINNER), const),
        pl.BlockSpec((1, DINNER), const),
    ]


def _ssd_prompt(xbc, dt, z, *params):
    nc = SEQ // CHUNK
    blk = lambda b, c: (b * nc + c, 0)
    const = lambda b, c: (0, 0)
    return pl.pallas_call(
        _ssd_prompt_kernel,
        grid=(BATCH, nc),
        in_specs=[
            pl.BlockSpec((CHUNK, CONV_DIM), blk),
            pl.BlockSpec((CHUNK, LANES), blk),
            pl.BlockSpec((CHUNK, DINNER), blk),
        ] + _ssm_param_specs(const),
        out_specs=[
            pl.BlockSpec((CHUNK, DINNER), blk),
            pl.BlockSpec((None, DINNER, STATE), lambda b, c: (b, 0, 0)),
        ],
        out_shape=[
            jax.ShapeDtypeStruct((N_TOK, DINNER), BF16),
            jax.ShapeDtypeStruct((BATCH, DINNER, STATE), F32),
        ],
        scratch_shapes=[pltpu.VMEM((CHUNK + SUBLANES, CONV_DIM), F32), pltpu.VMEM((STATE, DINNER), F32)],
        compiler_params=_cparams(("parallel", "arbitrary")),
        name="ssd_prompt",
    )(xbc, dt, z, *params)


def _ssd_sample_kernel(yn_any, xbc_ref, cst_ref, dt_ref, z_ref, h0_ref, w8_ref, cb_ref, e_ref,
                       alog_ref, dx_ref, nrm_ref, yn_ref, hout_ref,
                       xp_ref, cm_ref, bt_ref, xcoef_ref, ecx_ref, y_ref):
    del yn_any
    j = pl.program_id(1)

    @pl.when(j == 0)
    def _():
        xp_ref[:, SUBLANES - (SSM_K - 1):SUBLANES, :] = cst_ref[...]
        xp_ref[:, SUBLANES:SUBLANES + DEC_SEQ, :] = xbc_ref[...]
        strips = _conv4_strips(lambda cols: xp_ref[:, :, cols], w8_ref, cb_ref)
        xs, cm, bm, y_diag, coef_x, ecum_x = _ssd_intra(
            strips, dt_ref[...], alog_ref[...], e_ref[...], DEC_SEQ)
        cm_ref[...] = cm
        for g in range(GROUPS):
            bt_ref[g] = bm[:, g * STATE:(g + 1) * STATE].T
        xcoef_ref[...] = (xs * coef_x).astype(BF16)
        ecx_ref[...] = ecum_x
        y_ref[...] = y_diag + dx_ref[...] * xs

    col = lax.broadcasted_iota(jnp.int32, (STATE, CHUNK), 1)
    base = j * (SEQ_PER_STEP * DEC_SEQ)
    for p in range(SEQ_PER_STEP):
        r0 = pl.multiple_of(base + p * DEC_SEQ, DEC_SEQ)
        rows = pl.ds(r0, DEC_SEQ)
        ht = h0_ref[p].T
        ht_b = ht.astype(BF16)
        ecx = ecx_ref[rows, :]
        cj = cm_ref[rows, :].astype(BF16)
        mine = jnp.logical_and(col >= r0, col < r0 + DEC_SEQ)
        y_off, new = [], []
        for g in range(GROUPS):
            gs = slice(g * GW, (g + 1) * GW)
            y_off.append(_dot(cj[:, g * STATE:(g + 1) * STATE], ht_b[:, gs]))
            bg_t = jnp.where(mine, bt_ref[g], 0.0).astype(BF16)
            new.append(_dot(bg_t, xcoef_ref[:, gs]))
        y_ref[rows, :] += jnp.concatenate(y_off, axis=1) * ecx
        ht_new = ht * ecx[DEC_SEQ - 1:DEC_SEQ, :] + jnp.concatenate(new, axis=1)
        hout_ref[p] = ht_new.T

    @pl.when(j == pl.num_programs(1) - 1)
    def _():
        _store_gated_norm(yn_ref, y_ref, z_ref, nrm_ref)


def _ssd_sample(yn_all, xbc3, cst, dt, z, h0, *params):
    nb = DEC_BATCH // SEGS
    off = N_PROMPT // CHUNK
    const = lambda i, j: (0, 0)
    tok = lambda i, j: (off + i, 0)
    seq = lambda i, j: (i * (SEGS // SEQ_PER_STEP) + j, 0, 0)
    return pl.pallas_call(
        _ssd_sample_kernel,
        grid=(nb, SEGS // SEQ_PER_STEP),
        in_specs=[
            pl.BlockSpec(memory_space=pl.ANY),
            pl.BlockSpec((SEGS, DEC_SEQ, CONV_DIM), lambda i, j: (off + i, 0, 0)),
            pl.BlockSpec((SEGS, SSM_K - 1, CONV_DIM), lambda i, j: (i, 0, 0)),
            pl.BlockSpec((CHUNK, LANES), tok),
            pl.BlockSpec((CHUNK, DINNER), tok),
            pl.BlockSpec((SEQ_PER_STEP, DINNER, STATE), seq),
        ] + _ssm_param_specs(const),
        out_specs=[
            pl.BlockSpec((CHUNK, DINNER), tok),
            pl.BlockSpec((SEQ_PER_STEP, DINNER, STATE), seq),
        ],
        out_shape=[
            jax.ShapeDtypeStruct((N_TOK, DINNER), BF16),
            jax.ShapeDtypeStruct((DEC_BATCH, DINNER, STATE), F32),
        ],
        scratch_shapes=[
            pltpu.VMEM((SEGS, SUBLANES + DEC_SEQ, CONV_DIM), F32),
            pltpu.VMEM((CHUNK, GROUPS * STATE), F32),
            pltpu.VMEM((GROUPS, STATE, CHUNK), F32),
            pltpu.VMEM((CHUNK, DINNER), BF16),
            pltpu.VMEM((CHUNK, DINNER), F32),
            pltpu.VMEM((CHUNK, DINNER), F32),
        ],
        input_output_aliases={0: 0},
        compiler_params=_cparams(("parallel", "arbitrary")),
        name="ssd_sample",
    )(yn_all, xbc3, cst, dt, z, h0, *params)


CC_PAD = 32
CC_FIRST = CC_PAD - (CC_K - 1)
CC_LN_ROWS = 32
CC_STRIP = 512


def _cc_kernel(*refs, n_seg, seg_len, carry, aliased):
    refs = list(refs)
    if aliased:
        refs.pop(0)
    u_ref = refs.pop(0)
    st_ref = None if carry else refs.pop(0)
    w8_ref, b_ref, g_ref, be_ref, o_ref, xp_ref, res_ref = refs
    if carry:
        @pl.when(pl.program_id(1) == 0)
        def _():
            xp_ref[:, 0:CC_PAD, :] = jnp.zeros((n_seg, CC_PAD, CC_DIM), F32)
    else:
        xp_ref[:, CC_FIRST:CC_PAD, :] = st_ref[...]
    xp_ref[:, CC_PAD:CC_PAD + seg_len, :] = u_ref[...]

    rb = min(seg_len, 64)
    n_shift = rb + CC_PAD - SUBLANES
    for s in range(n_seg):
        for t0 in range(0, seg_len, rb):
            for c0 in range(0, CC_DIM, CC_STRIP):
                cols = slice(c0, c0 + CC_STRIP)
                full = xp_ref[s, t0:t0 + rb + CC_PAD, cols]
                acc = None
                for b in range(SUBLANES):
                    xb = full if b == 0 else full[b:b + n_shift, :]
                    for a in range(CC_PAD // SUBLANES + 1):
                        k = SUBLANES * a + b - CC_FIRST
                        if k < 0 or k >= CC_K:
                            continue
                        win = xb[SUBLANES * a:SUBLANES * a + rb, :].reshape(rb // SUBLANES, SUBLANES, CC_STRIP)
                        t = w8_ref[k, :, cols][None] * win
                        acc = t if acc is None else acc + t
                row = s * seg_len + t0
                res_ref[row:row + rb, cols] = acc.reshape(rb, CC_STRIP) + b_ref[:, cols]

    for r0 in range(0, n_seg * seg_len, CC_LN_ROWS):
        v = res_ref[r0:r0 + CC_LN_ROWS, :]
        mu = jnp.mean(v, axis=-1, keepdims=True)
        d = v - mu
        var = jnp.mean(d * d, axis=-1, keepdims=True)
        yv = (d * lax.rsqrt(var + EPS)) * g_ref[...] + be_ref[...]
        o_ref[r0:r0 + CC_LN_ROWS, :] = jax.nn.silu(yv).astype(BF16)
    if carry:
        xp_ref[:, 0:CC_PAD, :] = xp_ref[:, seg_len:seg_len + CC_PAD, :]


def _cc_param_specs(const):
    return [
        pl.BlockSpec((CC_K, SUBLANES, CC_DIM), lambda *_: (0, 0, 0)),
        pl.BlockSpec((1, CC_DIM), const),
        pl.BlockSpec((1, CC_DIM), const),
        pl.BlockSpec((1, CC_DIM), const),
    ]


def _cc_prompt(u3, *params, tq):
    nc = SEQ // tq
    const = lambda i, c: (0, 0)
    return pl.pallas_call(
        functools.partial(_cc_kernel, n_seg=1, seg_len=tq, carry=True, aliased=False),
        grid=(BATCH, nc),
        in_specs=[pl.BlockSpec((1, tq, CC_DIM), lambda i, c: (i * nc + c, 0, 0))] + _cc_param_specs(const),
        out_specs=pl.BlockSpec((tq, CC_DIM), lambda i, c: (i * nc + c, 0)),
        out_shape=jax.ShapeDtypeStruct((N_TOK, CC_DIM), BF16),
        scratch_shapes=[pltpu.VMEM((1, CC_PAD + tq, CC_DIM), F32), pltpu.VMEM((tq, CC_DIM), F32)],
        compiler_params=_cparams(("parallel", "arbitrary")),
        name="cc_prompt",
    )(u3, *params)


def _cc_sample(uc_all, u3, st, *params, n_seg):
    const = lambda i: (0, 0)
    rows = n_seg * DEC_SEQ
    off = N_PROMPT // rows
    return pl.pallas_call(
        functools.partial(_cc_kernel, n_seg=n_seg, seg_len=DEC_SEQ, carry=False, aliased=True),
        grid=(DEC_BATCH // n_seg,),
        in_specs=[
            pl.BlockSpec(memory_space=pl.ANY),
            pl.BlockSpec((n_seg, DEC_SEQ, CC_DIM), lambda i: (off + i, 0, 0)),
            pl.BlockSpec((n_seg, CC_K - 1, CC_DIM), lambda i: (i, 0, 0)),
        ] + _cc_param_specs(const),
        out_specs=pl.BlockSpec((rows, CC_DIM), lambda i: (off + i, 0)),
        out_shape=jax.ShapeDtypeStruct((N_TOK, CC_DIM), BF16),
        scratch_shapes=[pltpu.VMEM((n_seg, CC_PAD + DEC_SEQ, CC_DIM), F32),
                        pltpu.VMEM((rows, CC_DIM), F32)],
        input_output_aliases={0: 0},
        compiler_params=_cparams(("parallel",)),
        name="cc_sample",
    )(uc_all, u3, st, *params)


def _merge_kernel(h_ref, yn_ref, uc_ref, wgs_ref, wgc_ref, wso_ref, wco_ref, o_ref):
    h = h_ref[...]
    g_ssd = jax.nn.sigmoid(_dot_nt(h, wgs_ref[...]))
    g_cc = jax.nn.sigmoid(_dot_nt(h, wgc_ref[...]))
    ssd_out = _dot(yn_ref[...], wso_ref[...])
    cc_out = _dot(uc_ref[...], wco_ref[...])
    o_ref[...] = (g_ssd * ssd_out + g_cc * cc_out).astype(BF16)


def _merge(h, yn, uc, wgate_t, wso, wco, *, tm=1024, tn=512):
    m = h.shape[0]
    row = lambda i, j: (i, 0)
    colw = lambda i, j: (0, j)
    roww = lambda i, j: (j, 0)
    roww2 = lambda i, j: (D_MODEL // tn + j, 0)
    return pl.pallas_call(
        _merge_kernel,
        grid=(m // tm, D_MODEL // tn),
        in_specs=[
            pl.BlockSpec((tm, D_MODEL), row),
            pl.BlockSpec((tm, DINNER), row),
            pl.BlockSpec((tm, CC_DIM), row),
            pl.BlockSpec((tn, D_MODEL), roww),
            pl.BlockSpec((tn, D_MODEL), roww2),
            pl.BlockSpec((DINNER, tn), colw),
            pl.BlockSpec((CC_DIM, tn), colw),
        ],
        out_specs=pl.BlockSpec((tm, tn), lambda i, j: (i, j)),
        out_shape=jax.ShapeDtypeStruct((m, D_MODEL), BF16),
        compiler_params=_cparams(("parallel", "arbitrary")),
        name="merge",
    )(h, yn, uc, wgate_t, wgate_t, wso, wco)


def _oproj_kernel(m_ref, x_ref, w_ref, nrm_ref, o_ref):
    o_ref[...] = x_ref[...] + _rms(_dot(m_ref[...], w_ref[...]), nrm_ref[...])


def _oproj(mg, x, w, nrm, *, tm=512):
    m = x.shape[0]
    row = lambda i: (i, 0)
    const = lambda i: (0, 0)
    return pl.pallas_call(
        _oproj_kernel,
        grid=(m // tm,),
        in_specs=[
            pl.BlockSpec((tm, D_MODEL), row),
            pl.BlockSpec((tm, D_MODEL), row),
            pl.BlockSpec((D_MODEL, D_MODEL), const),
            pl.BlockSpec((1, D_MODEL), const),
        ],
        out_specs=pl.BlockSpec((tm, D_MODEL), row),
        out_shape=jax.ShapeDtypeStruct((m, D_MODEL), F32),
        compiler_params=_cparams(("parallel",)),
        name="oproj",
    )(mg, x, w, nrm)


def _row(v):
    return v.reshape(1, -1).astype(F32)


def _pad_lanes(v):
    return jnp.pad(v.astype(F32), (0, LANES - v.shape[0])).reshape(1, LANES)


def _taps_on_sublanes(w):
    return jnp.broadcast_to(w.astype(F32)[:, None, :], (w.shape[0], SUBLANES, w.shape[1]))


def kernel(x_prompt, x_sample, state_ssm, state_ssm_conv, state_cc_conv, ffn1_pre_norm, ffn1_post_norm, ffn1_w_gate, ffn1_w_up, ffn1_w_down, mix_pre_norm, mix_post_norm, w_in, ssm_conv_w, ssm_conv_b, ssm_dt_bias, ssm_A_log, ssm_D, ssm_norm, w_ssd_out, cc_conv_w, cc_conv_b, cc_ln_g, cc_ln_b, w_cc_out, w_o, ffn2_pre_norm, ffn2_post_norm, ffn2_w_gate, ffn2_w_up, ffn2_w_down):
    l = 0
    npt, nst = N_PROMPT // FFN_TM, N_SAMPLE // FFN_TM

    f1 = (_row(ffn1_pre_norm[l]), _row(ffn1_post_norm[l]), _row(mix_pre_norm[l]))
    x1, h, w1g, w1u, w1d = _ffn(
        x_sample.reshape(N_SAMPLE, D_MODEL), 0, 1, *f1, ffn1_w_gate[l], ffn1_w_up[l], ffn1_w_down[l],
        out_rows=N_TOK, out_tile0=N_PROMPT // N_SAMPLE, emit_h=True, name="ffn1_sample",
        tm=N_SAMPLE, tf=FFN_TF // 2)
    w_in_t = w_in[l].T
    c_xbc = DINNER
    c_dt = c_xbc + CONV_DIM
    c_glu = c_dt + HEADS
    c_gate = c_glu + 2 * CC_DIM
    whole = lambda w: (w, 0, w.shape[0])
    later = (whole(ffn2_w_gate[l]), whole(ffn2_w_up[l]), whole(ffn2_w_down[l]),
             (w_in_t, 0, c_dt + LANES), (w_in_t, c_glu, 2 * CC_DIM), (w_in_t, c_gate, 2 * D_MODEL),
             whole(w_ssd_out[l]), whole(w_cc_out[l]), whole(w_o[l]))
    x1, h, w2g, w2u, w2d, wt, wt_glu, wt_gate, w_so, w_co, w_ob = _ffn(
        x_prompt.reshape(N_PROMPT, D_MODEL), 0, npt, *f1, w1g, w1u, w1d, out_rows=N_TOK, out_tile0=0,
        emit_h=True, prev=(x1, h), casts=later, name="ffn1_prompt")

    e01 = (jnp.arange(LANES)[:, None] == (jnp.arange(DINNER)[None, :] // HEADDIM)).astype(BF16)
    e01 = jnp.concatenate([e01] * 3, axis=0)
    d_x = jnp.repeat(ssm_D[l].astype(F32), HEADDIM).reshape(1, DINNER)

    z, dt = _proj_z_dt(h, wt, c_dt, _pad_lanes(ssm_dt_bias[l]))
    xbc = _mm(_mm_plain_kernel, h, [(wt, c_xbc)], CONV_DIM, F32, "proj_xbc", tn=WIDE_TN)
    u = _mm(_mm_glu_kernel, h, [(wt_glu, 0), (wt_glu, CC_DIM)], CC_DIM, F32, "proj_glu")

    ssm_params = (_taps_on_sublanes(ssm_conv_w[l]), _row(ssm_conv_b[l]), e01, _pad_lanes(ssm_A_log[l]),
                  d_x, _row(ssm_norm[l]))
    yn, ssm_p = _ssd_prompt(xbc, dt, z, *ssm_params)
    yn, ssm_s = _ssd_sample(yn, xbc.reshape(N_TOK // DEC_SEQ, DEC_SEQ, CONV_DIM), state_ssm_conv[l], dt, z,
                            state_ssm[l].reshape(DEC_BATCH, DINNER, STATE), *ssm_params)

    cc_params = (_taps_on_sublanes(cc_conv_w[l]), _row(cc_conv_b[l]), _row(cc_ln_g[l]), _row(cc_ln_b[l]))
    tq = 256
    uc = _cc_prompt(u.reshape(N_TOK // tq, tq, CC_DIM), *cc_params, tq=tq)
    uc = _cc_sample(uc, u.reshape(N_TOK // DEC_SEQ, DEC_SEQ, CC_DIM), state_cc_conv[l], *cc_params, n_seg=16)

    mg = _merge(h, yn, uc, wt_gate, w_so, w_co)
    x2 = _oproj(mg, x1, w_ob, _row(mix_post_norm[l]))

    f2 = (_row(ffn2_pre_norm[l]), _row(ffn2_post_norm[l]), _row(ffn2_post_norm[l]), w2g, w2u, w2d)
    (y_p,) = _ffn(x2, 0, npt, *f2, out_rows=N_PROMPT, out_tile0=0, emit_h=False, name="ffn2_prompt")
    (y_s,) = _ffn(x2, npt, nst, *f2, out_rows=N_SAMPLE, out_tile0=0, emit_h=False, name="ffn2_sample")

    y_prompt = y_p.reshape(BATCH, SEQ, D_MODEL)
    y_sample = y_s.reshape(DEC_BATCH, DEC_SEQ, D_MODEL)
    new_ssm_p = ssm_p.reshape(1, BATCH, HEADS, HEADDIM, STATE)
    new_ssm_s = ssm_s.reshape(1, DEC_BATCH, HEADS, HEADDIM, STATE)
    tail = lambda a, n: jnp.stack([a[(b + 1) * SEQ - n:(b + 1) * SEQ] for b in range(BATCH)])[None]
    new_sconv_p = tail(xbc, SSM_K - 1)
    new_cc_p = tail(u, CC_K - 1)
    xbc_s3 = xbc[N_PROMPT:].reshape(DEC_BATCH, DEC_SEQ, CONV_DIM)
    new_sconv_s = xbc_s3[:, DEC_SEQ - (SSM_K - 1):][None]
    u_s3 = u[N_PROMPT:].reshape(DEC_BATCH, DEC_SEQ, CC_DIM)
    new_cc_s = jnp.concatenate([state_cc_conv[l][:, DEC_SEQ:], u_s3], axis=1)[None]
    return (y_prompt, y_sample, new_ssm_p, new_sconv_p, new_cc_p, new_ssm_s, new_sconv_s, new_cc_s)
```

```python
import functools

import jax
import jax.numpy as jnp
from jax import lax
from jax.experimental import pallas as pl
from jax.experimental.pallas import tpu as pltpu

F32 = jnp.float32
BF16 = jnp.bfloat16

D_MODEL = 2048
D_FF = 5632
BATCH, SEQ = 4, 2048
DEC_BATCH, DEC_SEQ = 128, 8
N_PROMPT = BATCH * SEQ
N_SAMPLE = DEC_BATCH * DEC_SEQ
N_TOK = N_PROMPT + N_SAMPLE
HEADS, HEADDIM, GROUPS, STATE = 32, 64, 4, 128
HPG = HEADS // GROUPS
DINNER = HEADS * HEADDIM
GW = HPG * HEADDIM
SSM_K = 4
CONV_DIM = DINNER + 2 * GROUPS * STATE
CC_DIM = D_MODEL // 2
CC_K = 31
EPS = 1e-6

LANES = 128
SUBLANES = 8
CHUNK = 128
SEGS = CHUNK // DEC_SEQ
SEQ_PER_STEP = 4
VMEM_LIMIT = 48 * 1024 * 1024
FFN_VMEM_LIMIT = 58 * 1024 * 1024


def _cparams(sem, vmem=VMEM_LIMIT):
    return pltpu.CompilerParams(dimension_semantics=sem, vmem_limit_bytes=vmem)


def _rms(x, g):
    ms = jnp.mean(x * x, axis=-1, keepdims=True)
    return (x * lax.rsqrt(ms + EPS)) * g


def _dot(a, b):
    return jnp.dot(a, b, preferred_element_type=F32)


def _split3(x):
    hi = x.astype(BF16)
    r = x - hi.astype(F32)
    mid = r.astype(BF16)
    lo = (r - mid.astype(F32)).astype(BF16)
    return hi, mid, lo


def _dot01_rhs(x, m01x3):
    return _dot(jnp.concatenate(_split3(x), axis=1), m01x3)


def _dot01_lhs(m01, x):
    return _dot(jnp.concatenate([m01] * 3, axis=1), jnp.concatenate(_split3(x), axis=0))


BF16_ROWS = 16
NORM_ROWS = 8
FFN_TM = 512
FFN_TF = 512


def _ffn_kernel(*refs, n_prev, n_cast, emit_h, f32_weights):
    refs = refs[n_prev:]
    x_ref, pre_ref, post_ref, nxt_ref, wg_ref, wu_ref, wd_ref = refs[:7]
    src_refs = refs[7:7 + n_cast]
    outs = refs[7 + n_cast:]
    y_ref = outs[0]
    ho_ref = outs[1] if emit_h else None
    dst_refs = outs[1 + emit_h:1 + emit_h + n_cast]
    own_refs = outs[1 + emit_h + n_cast:-2]
    h_scr, acc_scr = outs[-2:]
    j = pl.program_id(1)

    @pl.when(j == 0)
    def _():
        h_scr[...] = _rms(x_ref[...], pre_ref[...]).astype(BF16)
        acc_scr[...] = jnp.zeros_like(acc_scr)

    _cast_side_job(src_refs, dst_refs)

    wg, wu, wd = wg_ref[...], wu_ref[...], wd_ref[...]
    if f32_weights:
        wg, wu, wd = wg.astype(BF16), wu.astype(BF16), wd.astype(BF16)
        for own_ref, w in zip(own_refs, (wg, wu, wd)):
            own_ref[...] = w
    h = h_scr[...]
    g = _dot(h, wg)
    u = _dot(h, wu)
    a = (jax.nn.silu(g) * u).astype(BF16)
    acc_scr[...] += _dot(a, wd)

    @pl.when(j == pl.num_programs(1) - 1)
    def _():
        if emit_h:
            xn = x_ref[...] + 0.5 * _rms(acc_scr[...], post_ref[...])
            y_ref[...] = xn
            ho_ref[...] = _rms(xn, nxt_ref[...]).astype(BF16)
        else:
            for r in range(0, x_ref.shape[0], NORM_ROWS):
                rows = slice(r, r + NORM_ROWS)
                y_ref[rows, :] = x_ref[rows, :] + 0.5 * _rms(acc_scr[rows, :], post_ref[...])


def _cast_chunk_rows(rows, n_steps):
    return BF16_ROWS * pl.cdiv(pl.cdiv(rows, BF16_ROWS), n_steps)


def _cast_streams(casts, n_outer, n_inner):
    in_specs, out_specs, out_shape = [], [], []
    for w, row0, rows in casts:
        cols = w.shape[1]
        rc = _cast_chunk_rows(rows, n_outer * n_inner)
        assert row0 % rc == 0
        chunk = lambda i, j, first, last: (first + jnp.minimum(i * n_inner + j, last), 0)
        last = pl.cdiv(rows, rc) - 1
        in_specs.append(pl.BlockSpec((rc, cols), functools.partial(chunk, first=row0 // rc, last=last)))
        out_specs.append(pl.BlockSpec((rc, cols), functools.partial(chunk, first=0, last=last)))
        out_shape.append(jax.ShapeDtypeStruct((rows, cols), BF16))
    return in_specs, out_specs, out_shape


def _cast_side_job(src_refs, dst_refs):
    for src_ref, dst_ref in zip(src_refs, dst_refs):
        dst_ref[...] = src_ref[...].astype(BF16)


def _ffn(x, x_tile0, n_tiles, pre, post, nxt, wg, wu, wd, *, out_rows, out_tile0, emit_h,
         prev=(), casts=(), name, tm=FFN_TM, tf=FFN_TF):
    nj = D_FF // tf
    vec = pl.BlockSpec((1, D_MODEL), lambda i, j: (0, 0))
    mode = dict(pipeline_mode=pl.Buffered(1)) if n_tiles == 1 else {}
    tile = lambda t0: pl.BlockSpec((tm, D_MODEL), functools.partial(lambda i, j, t: (t + i, 0), t=t0), **mode)
    out_shape = [jax.ShapeDtypeStruct((out_rows, D_MODEL), F32)]
    out_specs = [tile(out_tile0)]
    if emit_h:
        out_shape.append(jax.ShapeDtypeStruct((out_rows, D_MODEL), BF16))
        out_specs.append(tile(out_tile0))
    cast_specs, cast_out_specs, cast_out_shape = _cast_streams(casts, n_tiles, nj)
    out_specs += cast_out_specs
    out_shape += cast_out_shape
    w_specs = [
        pl.BlockSpec((D_MODEL, tf), lambda i, j: (0, j)),
        pl.BlockSpec((D_MODEL, tf), lambda i, j: (0, j)),
        pl.BlockSpec((tf, D_MODEL), lambda i, j: (j, 0)),
    ]
    f32_weights = wg.dtype == F32
    if f32_weights:
        once = lambda i, j: jnp.where(i == 0, j, nj - 1)
        out_shape += [jax.ShapeDtypeStruct(w.shape, BF16) for w in (wg, wu, wd)]
        out_specs += [
            pl.BlockSpec((D_MODEL, tf), lambda i, j: (0, once(i, j))),
            pl.BlockSpec((D_MODEL, tf), lambda i, j: (0, once(i, j))),
            pl.BlockSpec((tf, D_MODEL), lambda i, j: (once(i, j), 0)),
        ]
    return pl.pallas_call(
        functools.partial(_ffn_kernel, n_prev=len(prev), n_cast=len(casts), emit_h=emit_h,
                          f32_weights=f32_weights),
        grid=(n_tiles, nj),
        in_specs=[pl.BlockSpec(memory_space=pl.ANY)] * len(prev)
        + [tile(x_tile0), vec, vec, vec] + w_specs + cast_specs,
        out_specs=out_specs,
        out_shape=out_shape,
        scratch_shapes=[pltpu.VMEM((tm, D_MODEL), BF16), pltpu.VMEM((tm, D_MODEL), F32)],
        input_output_aliases={k: k for k in range(len(prev))},
        compiler_params=_cparams(("arbitrary", "arbitrary"), FFN_VMEM_LIMIT),
        name=name,
    )(*prev, x, pre, post, nxt, wg, wu, wd, *[w for w, _, _ in casts])


WIDE_TN = 1024


def _dot_nt(a, b_t):
    return lax.dot_general(a, b_t, (((1,), (1,)), ((), ())), preferred_element_type=F32)


def _mm_plain_kernel(h_ref, w_ref, o_ref):
    o_ref[...] = _dot_nt(h_ref[...], w_ref[...]).astype(o_ref.dtype)


def _z_dt_kernel(h_ref, wz_ref, wdt_ref, b_ref, z_ref, dt_ref):
    h = h_ref[...]
    z_ref[...] = _dot_nt(h, wz_ref[...])

    @pl.when(pl.program_id(1) == 0)
    def _():
        dt = jax.nn.softplus(_dot_nt(h, wdt_ref[...]) + b_ref[...])
        lane = lax.broadcasted_iota(jnp.int32, dt.shape, 1)
        dt_ref[...] = jnp.where(lane < HEADS, dt, 0.0)


def _proj_z_dt(h, wt, dt_row0, dt_bias, *, tm=1024, tn=WIDE_TN):
    m, k = h.shape
    assert dt_row0 % LANES == 0
    return pl.pallas_call(
        _z_dt_kernel,
        grid=(m // tm, DINNER // tn),
        in_specs=[
            pl.BlockSpec((tm, k), lambda i, j: (i, 0)),
            pl.BlockSpec((tn, k), lambda i, j: (j, 0)),
            pl.BlockSpec((LANES, k), lambda i, j: (dt_row0 // LANES, 0)),
            pl.BlockSpec((1, LANES), lambda i, j: (0, 0)),
        ],
        out_specs=[
            pl.BlockSpec((tm, tn), lambda i, j: (i, j)),
            pl.BlockSpec((tm, LANES), lambda i, j: (i, 0)),
        ],
        out_shape=[jax.ShapeDtypeStruct((m, DINNER), F32), jax.ShapeDtypeStruct((m, LANES), F32)],
        compiler_params=_cparams(("parallel", "arbitrary")),
        name="proj_z_dt",
    )(h, wt, wt, dt_bias)


def _mm_glu_kernel(h_ref, wa_ref, wb_ref, o_ref):
    h = h_ref[...]
    o_ref[...] = _dot_nt(h, wa_ref[...]) * jax.nn.sigmoid(_dot_nt(h, wb_ref[...]))


def _mm(kernel, h, ws, n_out, out_dtype, name, *, tm=1024, tn=512):
    m, k = h.shape
    tn = min(tn, n_out)
    grid = (m // tm, n_out // tn)
    in_specs = [pl.BlockSpec((tm, k), lambda i, j: (i, 0))]
    for _, r0 in ws:
        assert r0 % tn == 0
        in_specs.append(pl.BlockSpec((tn, k), functools.partial(lambda i, j, o: (o + j, 0), o=r0 // tn)))
    return pl.pallas_call(
        kernel,
        grid=grid,
        in_specs=in_specs,
        out_specs=pl.BlockSpec((tm, tn), lambda i, j: (i, j)),
        out_shape=jax.ShapeDtypeStruct((m, n_out), out_dtype),
        compiler_params=_cparams(("parallel", "arbitrary")),
        name=name,
    )(h, *[w for w, _ in ws])


def _block_masks(seg_len):
    t = lax.broadcasted_iota(jnp.int32, (CHUNK, CHUNK), 0)
    s = lax.broadcasted_iota(jnp.int32, (CHUNK, CHUNK), 1)
    if seg_len == CHUNK:
        same = t >= 0
    else:
        sh = seg_len.bit_length() - 1
        same = (t >> sh) == (s >> sh)
    causal = jnp.logical_and(same, s <= t)
    return same, causal


CONV_STRIP = 512


def _conv4_silu(full, w8_ref, cb_ref, cols):
    n = full.shape[-2] - SUBLANES
    width = full.shape[-1]
    acc = None
    for k in range(SSM_K):
        r0 = SUBLANES - (SSM_K - 1) + k
        win = full[..., r0:r0 + n, :].reshape(CHUNK // SUBLANES, SUBLANES, width)
        t = w8_ref[k, :, cols][None] * win
        acc = t if acc is None else acc + t
    return jax.nn.silu(acc.reshape(CHUNK, width) + cb_ref[:, cols])


def _conv4_strips(load_strip, w8_ref, cb_ref):
    strips = []
    for c0 in range(0, CONV_DIM, CONV_STRIP):
        cols = slice(c0, c0 + CONV_STRIP)
        strips.append(_conv4_silu(load_strip(cols), w8_ref, cb_ref, cols))
    return strips


def _ssd_intra(strips, dt, alog_row, e01, seg_len):
    same, causal = _block_masks(seg_len)
    n_x = DINNER // CONV_STRIP
    xs = jnp.concatenate(strips[:n_x], axis=1)
    bm = strips[n_x]
    cm = strips[n_x + 1]
    a_row = -jnp.exp(alog_row)
    dta = dt * a_row
    causal01 = jnp.where(causal, 1.0, 0.0).astype(BF16)
    same01 = jnp.where(same, 1.0, 0.0).astype(BF16)
    cum = _dot01_lhs(causal01, dta)
    cum_end = _dot01_lhs(same01, dta)
    cum_t = cum.T
    dt_t = dt.T
    ecum_x = _dot01_rhs(jnp.exp(cum), e01)
    coef_x = _dot01_rhs(jnp.exp(cum_end - cum) * dt, e01)

    lane = lax.broadcasted_iota(jnp.int32, (CHUNK, LANES), 1)
    first_head = lane < HEADDIM
    ys = []
    for g in range(GROUPS):
        cg = cm[:, g * STATE:(g + 1) * STATE].astype(BF16)
        bg = bm[:, g * STATE:(g + 1) * STATE].astype(BF16)
        cb = lax.dot_general(cg, bg, (((1,), (1,)), ((), ())), preferred_element_type=F32)
        for p in range(HPG // 2):
            ws = []
            for e in (g * HPG + 2 * p, g * HPG + 2 * p + 1):
                seg = cum[:, e:e + 1] - cum_t[e:e + 1, :]
                dec = jnp.exp(jnp.where(causal, seg, -jnp.inf))
                ws.append(((cb * dec) * dt_t[e:e + 1, :]).astype(BF16))
            w2 = jnp.concatenate(ws, axis=1)
            col = (g * HPG + 2 * p) * HEADDIM
            x2 = xs[:, col:col + LANES]
            r2 = jnp.concatenate([jnp.where(first_head, x2, 0.0),
                                  jnp.where(first_head, 0.0, x2)], axis=0).astype(BF16)
            ys.append(_dot(w2, r2))
    y_diag = jnp.concatenate(ys, axis=1)
    return xs, cm, bm, y_diag, coef_x, ecum_x


def _store_gated_norm(yn_ref, y, z_ref, nrm_ref):
    for r in range(0, CHUNK, BF16_ROWS):
        rows = slice(r, r + BF16_ROWS)
        yn_ref[rows, :] = _rms(y[rows, :] * jax.nn.silu(z_ref[rows, :]), nrm_ref[...]).astype(BF16)


def _ssd_prompt_kernel(xbc_ref, dt_ref, z_ref, w8_ref, cb_ref, e_ref, alog_ref, dx_ref, nrm_ref,
                       yn_ref, hout_ref, xp_ref, ht_ref):
    c = pl.program_id(1)

    @pl.when(c == 0)
    def _():
        xp_ref[0:SUBLANES, :] = jnp.zeros((SUBLANES, CONV_DIM), F32)
        ht_ref[...] = jnp.zeros_like(ht_ref)

    xp_ref[SUBLANES:SUBLANES + CHUNK, :] = xbc_ref[...]
    strips = _conv4_strips(lambda cols: xp_ref[:, cols], w8_ref, cb_ref)
    xp_ref[0:SUBLANES, :] = xbc_ref[CHUNK - SUBLANES:CHUNK, :]

    xs, cm, bm, y_diag, coef_x, ecum_x = _ssd_intra(strips, dt_ref[...], alog_ref[...], e_ref[...], CHUNK)
    xcoef = (xs * coef_x).astype(BF16)
    dec_row = ecum_x[CHUNK - 1:CHUNK, :]
    y_off = []
    for g in range(GROUPS):
        gs = slice(g * GW, (g + 1) * GW)
        ht_g = ht_ref[:, gs]
        cg = cm[:, g * STATE:(g + 1) * STATE].astype(BF16)
        y_off.append(_dot(cg, ht_g.astype(BF16)))
        bg_t = bm[:, g * STATE:(g + 1) * STATE].T.astype(BF16)
        ht_ref[:, gs] = ht_g * dec_row[:, gs] + _dot(bg_t, xcoef[:, gs])
    y = y_diag + jnp.concatenate(y_off, axis=1) * ecum_x + dx_ref[...] * xs
    _store_gated_norm(yn_ref, y, z_ref, nrm_ref)

    @pl.when(c == pl.num_programs(1) - 1)
    def _():
        hout_ref[...] = ht_ref[...].T


def _ssm_param_specs(const):
    return [
        pl.BlockSpec((SSM_K, SUBLANES, CONV_DIM), lambda *_: (0, 0, 0)),
        pl.BlockSpec((1, CONV_DIM), const),
        pl.BlockSpec((3 * LANES, DINNER), const),
        pl.BlockSpec((1, LANES), const),
        pl.BlockSpec((1, DINNER), const),
        pl.BlockSpec((1, DINNER), const),
    ]


def _ssd_prompt(xbc, dt, z, *params):
    nc = SEQ // CHUNK
    blk = lambda b, c: (b * nc + c, 0)
    const = lambda b, c: (0, 0)
    return pl.pallas_call(
        _ssd_prompt_kernel,
        grid=(BATCH, nc),
        in_specs=[
            pl.BlockSpec((CHUNK, CONV_DIM), blk),
            pl.BlockSpec((CHUNK, LANES), blk),
            pl.BlockSpec((CHUNK, DINNER), blk),
        ] + _ssm_param_specs(const),
        out_specs=[
            pl.BlockSpec((CHUNK, DINNER), blk),
            pl.BlockSpec((None, DINNER, STATE), lambda b, c: (b, 0, 0)),
        ],
        out_shape=[
            jax.ShapeDtypeStruct((N_TOK, DINNER), BF16),
            jax.ShapeDtypeStruct((BATCH, DINNER, STATE), F32),
        ],
        scratch_shapes=[pltpu.VMEM((CHUNK + SUBLANES, CONV_DIM), F32), pltpu.VMEM((STATE, DINNER), F32)],
        compiler_params=_cparams(("parallel", "arbitrary")),
        name="ssd_prompt",
    )(xbc, dt, z, *params)


def _ssd_sample_kernel(yn_any, xbc_ref, cst_ref, dt_ref, z_ref, h0_ref, w8_ref, cb_ref, e_ref,
                       alog_ref, dx_ref, nrm_ref, yn_ref, hout_ref,
                       xp_ref, cm_ref, bt_ref, xcoef_ref, ecx_ref, y_ref):
    del yn_any
    j = pl.program_id(1)

    @pl.when(j == 0)
    def _():
        xp_ref[:, SUBLANES - (SSM_K - 1):SUBLANES, :] = cst_ref[...]
        xp_ref[:, SUBLANES:SUBLANES + DEC_SEQ, :] = xbc_ref[...]
        strips = _conv4_strips(lambda cols: xp_ref[:, :, cols], w8_ref, cb_ref)
        xs, cm, bm, y_diag, coef_x, ecum_x = _ssd_intra(
            strips, dt_ref[...], alog_ref[...], e_ref[...], DEC_SEQ)
        cm_ref[...] = cm
        for g in range(GROUPS):
            bt_ref[g] = bm[:, g * STATE:(g + 1) * STATE].T
        xcoef_ref[...] = (xs * coef_x).astype(BF16)
        ecx_ref[...] = ecum_x
        y_ref[...] = y_diag + dx_ref[...] * xs

    col = lax.broadcasted_iota(jnp.int32, (STATE, CHUNK), 1)
    base = j * (SEQ_PER_STEP * DEC_SEQ)
    for p in range(SEQ_PER_STEP):
        r0 = pl.multiple_of(base + p * DEC_SEQ, DEC_SEQ)
        rows = pl.ds(r0, DEC_SEQ)
        ht = h0_ref[p].T
        ht_b = ht.astype(BF16)
        ecx = ecx_ref[rows, :]
        cj = cm_ref[rows, :].astype(BF16)
        mine = jnp.logical_and(col >= r0, col < r0 + DEC_SEQ)
        y_off, new = [], []
        for g in range(GROUPS):
            gs = slice(g * GW, (g + 1) * GW)
            y_off.append(_dot(cj[:, g * STATE:(g + 1) * STATE], ht_b[:, gs]))
            bg_t = jnp.where(mine, bt_ref[g], 0.0).astype(BF16)
            new.append(_dot(bg_t, xcoef_ref[:, gs]))
        y_ref[rows, :] += jnp.concatenate(y_off, axis=1) * ecx
        ht_new = ht * ecx[DEC_SEQ - 1:DEC_SEQ, :] + jnp.concatenate(new, axis=1)
        hout_ref[p] = ht_new.T

    @pl.when(j == pl.num_programs(1) - 1)
    def _():
        _store_gated_norm(yn_ref, y_ref, z_ref, nrm_ref)


def _ssd_sample(yn_all, xbc3, cst, dt, z, h0, *params):
    nb = DEC_BATCH // SEGS
    off = N_PROMPT // CHUNK
    const = lambda i, j: (0, 0)
    tok = lambda i, j: (off + i, 0)
    seq = lambda i, j: (i * (SEGS // SEQ_PER_STEP) + j, 0, 0)
    return pl.pallas_call(
        _ssd_sample_kernel,
        grid=(nb, SEGS // SEQ_PER_STEP),
        in_specs=[
            pl.BlockSpec(memory_space=pl.ANY),
            pl.BlockSpec((SEGS, DEC_SEQ, CONV_DIM), lambda i, j: (off + i, 0, 0)),
            pl.BlockSpec((SEGS, SSM_K - 1, CONV_DIM), lambda i, j: (i, 0, 0)),
            pl.BlockSpec((CHUNK, LANES), tok),
            pl.BlockSpec((CHUNK, DINNER), tok),
            pl.BlockSpec((SEQ_PER_STEP, DINNER, STATE), seq),
        ] + _ssm_param_specs(const),
        out_specs=[
            pl.BlockSpec((CHUNK, DINNER), tok),
            pl.BlockSpec((SEQ_PER_STEP, DINNER, STATE), seq),
        ],
        out_shape=[
            jax.ShapeDtypeStruct((N_TOK, DINNER), BF16),
            jax.ShapeDtypeStruct((DEC_BATCH, DINNER, STATE), F32),
        ],
        scratch_shapes=[
            pltpu.VMEM((SEGS, SUBLANES + DEC_SEQ, CONV_DIM), F32),
            pltpu.VMEM((CHUNK, GROUPS * STATE), F32),
            pltpu.VMEM((GROUPS, STATE, CHUNK), F32),
            pltpu.VMEM((CHUNK, DINNER), BF16),
            pltpu.VMEM((CHUNK, DINNER), F32),
            pltpu.VMEM((CHUNK, DINNER), F32),
        ],
        input_output_aliases={0: 0},
        compiler_params=_cparams(("parallel", "arbitrary")),
        name="ssd_sample",
    )(yn_all, xbc3, cst, dt, z, h0, *params)


CC_PAD = 32
CC_FIRST = CC_PAD - (CC_K - 1)
CC_LN_ROWS = 32
CC_STRIP = 512


def _cc_kernel(*refs, n_seg, seg_len, carry, aliased):
    refs = list(refs)
    if aliased:
        refs.pop(0)
    u_ref = refs.pop(0)
    st_ref = None if carry else refs.pop(0)
    w8_ref, b_ref, g_ref, be_ref, o_ref, xp_ref, res_ref = refs
    if carry:
        @pl.when(pl.program_id(1) == 0)
        def _():
            xp_ref[:, 0:CC_PAD, :] = jnp.zeros((n_seg, CC_PAD, CC_DIM), F32)
    else:
        xp_ref[:, CC_FIRST:CC_PAD, :] = st_ref[...]
    xp_ref[:, CC_PAD:CC_PAD + seg_len, :] = u_ref[...]

    rb = min(seg_len, 64)
    n_shift = rb + CC_PAD - SUBLANES
    for s in range(n_seg):
        for t0 in range(0, seg_len, rb):
            for c0 in range(0, CC_DIM, CC_STRIP):
                cols = slice(c0, c0 + CC_STRIP)
                full = xp_ref[s, t0:t0 + rb + CC_PAD, cols]
                acc = None
                for b in range(SUBLANES):
                    xb = full if b == 0 else full[b:b + n_shift, :]
                    for a in range(CC_PAD // SUBLANES + 1):
                        k = SUBLANES * a + b - CC_FIRST
                        if k < 0 or k >= CC_K:
                            continue
                        win = xb[SUBLANES * a:SUBLANES * a + rb, :].reshape(rb // SUBLANES, SUBLANES, CC_STRIP)
                        t = w8_ref[k, :, cols][None] * win
                        acc = t if acc is None else acc + t
                row = s * seg_len + t0
                res_ref[row:row + rb, cols] = acc.reshape(rb, CC_STRIP) + b_ref[:, cols]

    for r0 in range(0, n_seg * seg_len, CC_LN_ROWS):
        v = res_ref[r0:r0 + CC_LN_ROWS, :]
        mu = jnp.mean(v, axis=-1, keepdims=True)
        d = v - mu
        var = jnp.mean(d * d, axis=-1, keepdims=True)
        yv = (d * lax.rsqrt(var + EPS)) * g_ref[...] + be_ref[...]
        o_ref[r0:r0 + CC_LN_ROWS, :] = jax.nn.silu(yv).astype(BF16)
    if carry:
        xp_ref[:, 0:CC_PAD, :] = xp_ref[:, seg_len:seg_len + CC_PAD, :]


def _cc_param_specs(const):
    return [
        pl.BlockSpec((CC_K, SUBLANES, CC_DIM), lambda *_: (0, 0, 0)),
        pl.BlockSpec((1, CC_DIM), const),
        pl.BlockSpec((1, CC_DIM), const),
        pl.BlockSpec((1, CC_DIM), const),
    ]


def _cc_prompt(u3, *params, tq):
    nc = SEQ // tq
    const = lambda i, c: (0, 0)
    return pl.pallas_call(
        functools.partial(_cc_kernel, n_seg=1, seg_len=tq, carry=True, aliased=False),
        grid=(BATCH, nc),
        in_specs=[pl.BlockSpec((1, tq, CC_DIM), lambda i, c: (i * nc + c, 0, 0))] + _cc_param_specs(const),
        out_specs=pl.BlockSpec((tq, CC_DIM), lambda i, c: (i * nc + c, 0)),
        out_shape=jax.ShapeDtypeStruct((N_TOK, CC_DIM), BF16),
        scratch_shapes=[pltpu.VMEM((1, CC_PAD + tq, CC_DIM), F32), pltpu.VMEM((tq, CC_DIM), F32)],
        compiler_params=_cparams(("parallel", "arbitrary")),
        name="cc_prompt",
    )(u3, *params)


def _cc_sample(uc_all, u3, st, *params, n_seg):
    const = lambda i: (0, 0)
    rows = n_seg * DEC_SEQ
    off = N_PROMPT // rows
    return pl.pallas_call(
        functools.partial(_cc_kernel, n_seg=n_seg, seg_len=DEC_SEQ, carry=False, aliased=True),
        grid=(DEC_BATCH // n_seg,),
        in_specs=[
            pl.BlockSpec(memory_space=pl.ANY),
            pl.BlockSpec((n_seg, DEC_SEQ, CC_DIM), lambda i: (off + i, 0, 0)),
            pl.BlockSpec((n_seg, CC_K - 1, CC_DIM), lambda i: (i, 0, 0)),
        ] + _cc_param_specs(const),
        out_specs=pl.BlockSpec((rows, CC_DIM), lambda i: (off + i, 0)),
        out_shape=jax.ShapeDtypeStruct((N_TOK, CC_DIM), BF16),
        scratch_shapes=[pltpu.VMEM((n_seg, CC_PAD + DEC_SEQ, CC_DIM), F32),
                        pltpu.VMEM((rows, CC_DIM), F32)],
        input_output_aliases={0: 0},
        compiler_params=_cparams(("parallel",)),
        name="cc_sample",
    )(uc_all, u3, st, *params)


def _merge_kernel(h_ref, yn_ref, uc_ref, wgs_ref, wgc_ref, wso_ref, wco_ref, o_ref):
    h = h_ref[...]
    g_ssd = jax.nn.sigmoid(_dot_nt(h, wgs_ref[...]))
    g_cc = jax.nn.sigmoid(_dot_nt(h, wgc_ref[...]))
    ssd_out = _dot(yn_ref[...], wso_ref[...])
    cc_out = _dot(uc_ref[...], wco_ref[...])
    o_ref[...] = (g_ssd * ssd_out + g_cc * cc_out).astype(BF16)


def _merge(h, yn, uc, wgate_t, wso, wco, *, tm=1024, tn=512):
    m = h.shape[0]
    row = lambda i, j: (i, 0)
    colw = lambda i, j: (0, j)
    roww = lambda i, j: (j, 0)
    roww2 = lambda i, j: (D_MODEL // tn + j, 0)
    return pl.pallas_call(
        _merge_kernel,
        grid=(m // tm, D_MODEL // tn),
        in_specs=[
            pl.BlockSpec((tm, D_MODEL), row),
            pl.BlockSpec((tm, DINNER), row),
            pl.BlockSpec((tm, CC_DIM), row),
            pl.BlockSpec((tn, D_MODEL), roww),
            pl.BlockSpec((tn, D_MODEL), roww2),
            pl.BlockSpec((DINNER, tn), colw),
            pl.BlockSpec((CC_DIM, tn), colw),
        ],
        out_specs=pl.BlockSpec((tm, tn), lambda i, j: (i, j)),
        out_shape=jax.ShapeDtypeStruct((m, D_MODEL), BF16),
        compiler_params=_cparams(("parallel", "arbitrary")),
        name="merge",
    )(h, yn, uc, wgate_t, wgate_t, wso, wco)


def _oproj_kernel(m_ref, x_ref, w_ref, nrm_ref, o_ref):
    o_ref[...] = x_ref[...] + _rms(_dot(m_ref[...], w_ref[...]), nrm_ref[...])


def _oproj(mg, x, w, nrm, *, tm=512):
    m = x.shape[0]
    row = lambda i: (i, 0)
    const = lambda i: (0, 0)
    return pl.pallas_call(
        _oproj_kernel,
        grid=(m // tm,),
        in_specs=[
            pl.BlockSpec((tm, D_MODEL), row),
            pl.BlockSpec((tm, D_MODEL), row),
            pl.BlockSpec((D_MODEL, D_MODEL), const),
            pl.BlockSpec((1, D_MODEL), const),
        ],
        out_specs=pl.BlockSpec((tm, D_MODEL), row),
        out_shape=jax.ShapeDtypeStruct((m, D_MODEL), F32),
        compiler_params=_cparams(("parallel",)),
        name="oproj",
    )(mg, x, w, nrm)


def _row(v):
    return v.reshape(1, -1).astype(F32)


def _pad_lanes(v):
    return jnp.pad(v.astype(F32), (0, LANES - v.shape[0])).reshape(1, LANES)


def _taps_on_sublanes(w):
    return jnp.broadcast_to(w.astype(F32)[:, None, :], (w.shape[0], SUBLANES, w.shape[1]))


def kernel(x_prompt, x_sample, state_ssm, state_ssm_conv, state_cc_conv, ffn1_pre_norm, ffn1_post_norm, ffn1_w_gate, ffn1_w_up, ffn1_w_down, mix_pre_norm, mix_post_norm, w_in, ssm_conv_w, ssm_conv_b, ssm_dt_bias, ssm_A_log, ssm_D, ssm_norm, w_ssd_out, cc_conv_w, cc_conv_b, cc_ln_g, cc_ln_b, w_cc_out, w_o, ffn2_pre_norm, ffn2_post_norm, ffn2_w_gate, ffn2_w_up, ffn2_w_down):
    l = 0
    npt = N_PROMPT // FFN_TM

    f1 = (_row(ffn1_pre_norm[l]), _row(ffn1_post_norm[l]), _row(mix_pre_norm[l]))
    x1, h, w1g, w1u, w1d = _ffn(
        x_sample.reshape(N_SAMPLE, D_MODEL), 0, 1, *f1, ffn1_w_gate[l], ffn1_w_up[l], ffn1_w_down[l],
        out_rows=N_TOK, out_tile0=N_PROMPT // N_SAMPLE, emit_h=True, name="ffn1_sample",
        tm=N_SAMPLE, tf=FFN_TF // 2)
    w_in_t = w_in[l].T
    c_xbc = DINNER
    c_dt = c_xbc + CONV_DIM
    c_glu = c_dt + HEADS
    c_gate = c_glu + 2 * CC_DIM
    whole = lambda w: (w, 0, w.shape[0])
    later = (whole(ffn2_w_gate[l]), whole(ffn2_w_up[l]), whole(ffn2_w_down[l]),
             (w_in_t, 0, c_dt + LANES), (w_in_t, c_glu, 2 * CC_DIM), (w_in_t, c_gate, 2 * D_MODEL),
             whole(w_ssd_out[l]), whole(w_cc_out[l]), whole(w_o[l]))
    x1, h, w2g, w2u, w2d, wt, wt_glu, wt_gate, w_so, w_co, w_ob = _ffn(
        x_prompt.reshape(N_PROMPT, D_MODEL), 0, npt, *f1, w1g, w1u, w1d, out_rows=N_TOK, out_tile0=0,
        emit_h=True, prev=(x1, h), casts=later, name="ffn1_prompt")

    e01 = (jnp.arange(LANES)[:, None] == (jnp.arange(DINNER)[None, :] // HEADDIM)).astype(BF16)
    e01 = jnp.concatenate([e01] * 3, axis=0)
    d_x = jnp.repeat(ssm_D[l].astype(F32), HEADDIM).reshape(1, DINNER)

    z, dt = _proj_z_dt(h, wt, c_dt, _pad_lanes(ssm_dt_bias[l]))
    xbc = _mm(_mm_plain_kernel, h, [(wt, c_xbc)], CONV_DIM, F32, "proj_xbc", tn=WIDE_TN)
    u = _mm(_mm_glu_kernel, h, [(wt_glu, 0), (wt_glu, CC_DIM)], CC_DIM, F32, "proj_glu", tn=WIDE_TN)

    ssm_params = (_taps_on_sublanes(ssm_conv_w[l]), _row(ssm_conv_b[l]), e01, _pad_lanes(ssm_A_log[l]),
                  d_x, _row(ssm_norm[l]))
    yn, ssm_p = _ssd_prompt(xbc, dt, z, *ssm_params)
    yn, ssm_s = _ssd_sample(yn, xbc.reshape(N_TOK // DEC_SEQ, DEC_SEQ, CONV_DIM), state_ssm_conv[l], dt, z,
                            state_ssm[l].reshape(DEC_BATCH, DINNER, STATE), *ssm_params)

    cc_params = (_taps_on_sublanes(cc_conv_w[l]), _row(cc_conv_b[l]), _row(cc_ln_g[l]), _row(cc_ln_b[l]))
    tq = 256
    uc = _cc_prompt(u.reshape(N_TOK // tq, tq, CC_DIM), *cc_params, tq=tq)
    uc = _cc_sample(uc, u.reshape(N_TOK // DEC_SEQ, DEC_SEQ, CC_DIM), state_cc_conv[l], *cc_params, n_seg=16)

    mg = _merge(h, yn, uc, wt_gate, w_so, w_co)
    x2 = _oproj(mg, x1, w_ob, _row(mix_post_norm[l]))

    f2 = (_row(ffn2_pre_norm[l]), _row(ffn2_post_norm[l]), _row(ffn2_post_norm[l]), w2g, w2u, w2d)
    (y_p,) = _ffn(x2, 0, npt, *f2, out_rows=N_PROMPT, out_tile0=0, emit_h=False, name="ffn2_prompt")
    (y_s,) = _ffn(x2, N_PROMPT // N_SAMPLE, 1, *f2, out_rows=N_SAMPLE, out_tile0=0, emit_h=False,
                  name="ffn2_sample", tm=N_SAMPLE)

    y_prompt = y_p.reshape(BATCH, SEQ, D_MODEL)
    y_sample = y_s.reshape(DEC_BATCH, DEC_SEQ, D_MODEL)
    new_ssm_p = ssm_p.reshape(1, BATCH, HEADS, HEADDIM, STATE)
    new_ssm_s = ssm_s.reshape(1, DEC_BATCH, HEADS, HEADDIM, STATE)
    tail = lambda a, n: jnp.stack([a[(b + 1) * SEQ - n:(b + 1) * SEQ] for b in range(BATCH)])[None]
    new_sconv_p = tail(xbc, SSM_K - 1)
    new_cc_p = tail(u, CC_K - 1)
    xbc_s3 = xbc[N_PROMPT:].reshape(DEC_BATCH, DEC_SEQ, CONV_DIM)
    new_sconv_s = xbc_s3[:, DEC_SEQ - (SSM_K - 1):][None]
    u_s3 = u[N_PROMPT:].reshape(DEC_BATCH, DEC_SEQ, CC_DIM)
    new_cc_s = jnp.concatenate([state_cc_conv[l][:, DEC_SEQ:], u_s3], axis=1)[None]
    return (y_prompt, y_sample, new_ssm_p, new_sconv_p, new_cc_p, new_ssm_s, new_sconv_s, new_cc_s)
```

```python
import functools

import jax
import jax.numpy as jnp
from jax import lax
from jax.experimental import pallas as pl
from jax.experimental.pallas import tpu as pltpu

F32 = jnp.float32
BF16 = jnp.bfloat16

D_MODEL = 2048
D_FF = 5632
BATCH, SEQ = 4, 2048
DEC_BATCH, DEC_SEQ = 128, 8
N_PROMPT = BATCH * SEQ
N_SAMPLE = DEC_BATCH * DEC_SEQ
N_TOK = N_PROMPT + N_SAMPLE
HEADS, HEADDIM, GROUPS, STATE = 32, 64, 4, 128
HPG = HEADS // GROUPS
DINNER = HEADS * HEADDIM
GW = HPG * HEADDIM
SSM_K = 4
CONV_DIM = DINNER + 2 * GROUPS * STATE
CC_DIM = D_MODEL // 2
CC_K = 31
EPS = 1e-6

LANES = 128
SUBLANES = 8
CHUNK = 128
SEGS = CHUNK // DEC_SEQ
SEQ_PER_STEP = 4
VMEM_LIMIT = 48 * 1024 * 1024
FFN_VMEM_LIMIT = 58 * 1024 * 1024


def _cparams(sem, vmem=VMEM_LIMIT):
    return pltpu.CompilerParams(dimension_semantics=sem, vmem_limit_bytes=vmem)


def _rms(x, g):
    ms = jnp.mean(x * x, axis=-1, keepdims=True)
    return (x * lax.rsqrt(ms + EPS)) * g


def _dot(a, b):
    return jnp.dot(a, b, preferred_element_type=F32)


def _split3(x):
    hi = x.astype(BF16)
    r = x - hi.astype(F32)
    mid = r.astype(BF16)
    lo = (r - mid.astype(F32)).astype(BF16)
    return hi, mid, lo


def _dot01_rhs(x, m01x3):
    return _dot(jnp.concatenate(_split3(x), axis=1), m01x3)


def _dot01_lhs(m01, x):
    return _dot(jnp.concatenate([m01] * 3, axis=1), jnp.concatenate(_split3(x), axis=0))


BF16_ROWS = 16
MXU_COLS = 256
NORM_ROWS = 8
FFN_TM = 512
FFN_TF = 512


def _ffn_kernel(*refs, n_prev, n_cast, emit_h, f32_weights):
    refs = refs[n_prev:]
    x_ref, pre_ref, post_ref, nxt_ref, wg_ref, wu_ref, wd_ref = refs[:7]
    src_refs = refs[7:7 + n_cast]
    outs = refs[7 + n_cast:]
    y_ref = outs[0]
    ho_ref = outs[1] if emit_h else None
    dst_refs = outs[1 + emit_h:1 + emit_h + n_cast]
    own_refs = outs[1 + emit_h + n_cast:-2]
    h_scr, acc_scr = outs[-2:]
    j = pl.program_id(1)

    @pl.when(j == 0)
    def _():
        h_scr[...] = _rms(x_ref[...], pre_ref[...]).astype(BF16)
        acc_scr[...] = jnp.zeros_like(acc_scr)

    _cast_side_job(src_refs, dst_refs)

    wg, wu, wd = wg_ref[...], wu_ref[...], wd_ref[...]
    if f32_weights:
        wg, wu, wd = wg.astype(BF16), wu.astype(BF16), wd.astype(BF16)
        for own_ref, w in zip(own_refs, (wg, wu, wd)):
            own_ref[...] = w
    h = h_scr[...]
    tf = wg.shape[1]
    half = tf // 2 if tf // 2 >= MXU_COLS else tf
    part = None
    for c0 in range(0, tf, half):
        g = _dot(h, wg[:, c0:c0 + half])
        u = _dot(h, wu[:, c0:c0 + half])
        a = (jax.nn.silu(g) * u).astype(BF16)
        d = _dot(a, wd[c0:c0 + half, :])
        part = d if part is None else part + d
    acc_scr[...] += part

    @pl.when(j == pl.num_programs(1) - 1)
    def _():
        if emit_h:
            xn = x_ref[...] + 0.5 * _rms(acc_scr[...], post_ref[...])
            y_ref[...] = xn
            ho_ref[...] = _rms(xn, nxt_ref[...]).astype(BF16)
        else:
            for r in range(0, x_ref.shape[0], NORM_ROWS):
                rows = slice(r, r + NORM_ROWS)
                y_ref[rows, :] = x_ref[rows, :] + 0.5 * _rms(acc_scr[rows, :], post_ref[...])


def _cast_chunk_rows(rows, n_steps):
    return BF16_ROWS * pl.cdiv(pl.cdiv(rows, BF16_ROWS), n_steps)


def _cast_streams(casts, n_outer, n_inner):
    in_specs, out_specs, out_shape = [], [], []
    for w, row0, rows in casts:
        cols = w.shape[1]
        rc = _cast_chunk_rows(rows, n_outer * n_inner)
        assert row0 % rc == 0
        chunk = lambda i, j, first, last: (first + jnp.minimum(i * n_inner + j, last), 0)
        last = pl.cdiv(rows, rc) - 1
        in_specs.append(pl.BlockSpec((rc, cols), functools.partial(chunk, first=row0 // rc, last=last)))
        out_specs.append(pl.BlockSpec((rc, cols), functools.partial(chunk, first=0, last=last)))
        out_shape.append(jax.ShapeDtypeStruct((rows, cols), BF16))
    return in_specs, out_specs, out_shape


def _cast_side_job(src_refs, dst_refs):
    for src_ref, dst_ref in zip(src_refs, dst_refs):
        dst_ref[...] = src_ref[...].astype(BF16)


def _ffn(x, x_tile0, n_tiles, pre, post, nxt, wg, wu, wd, *, out_rows, out_tile0, emit_h,
         prev=(), casts=(), name, tm=FFN_TM, tf=FFN_TF):
    nj = D_FF // tf
    vec = pl.BlockSpec((1, D_MODEL), lambda i, j: (0, 0))
    mode = dict(pipeline_mode=pl.Buffered(1)) if n_tiles == 1 else {}
    tile = lambda t0: pl.BlockSpec((tm, D_MODEL), functools.partial(lambda i, j, t: (t + i, 0), t=t0), **mode)
    out_shape = [jax.ShapeDtypeStruct((out_rows, D_MODEL), F32)]
    out_specs = [tile(out_tile0)]
    if emit_h:
        out_shape.append(jax.ShapeDtypeStruct((out_rows, D_MODEL), BF16))
        out_specs.append(tile(out_tile0))
    cast_specs, cast_out_specs, cast_out_shape = _cast_streams(casts, n_tiles, nj)
    out_specs += cast_out_specs
    out_shape += cast_out_shape
    w_specs = [
        pl.BlockSpec((D_MODEL, tf), lambda i, j: (0, j)),
        pl.BlockSpec((D_MODEL, tf), lambda i, j: (0, j)),
        pl.BlockSpec((tf, D_MODEL), lambda i, j: (j, 0)),
    ]
    f32_weights = wg.dtype == F32
    if f32_weights:
        once = lambda i, j: jnp.where(i == 0, j, nj - 1)
        out_shape += [jax.ShapeDtypeStruct(w.shape, BF16) for w in (wg, wu, wd)]
        out_specs += [
            pl.BlockSpec((D_MODEL, tf), lambda i, j: (0, once(i, j))),
            pl.BlockSpec((D_MODEL, tf), lambda i, j: (0, once(i, j))),
            pl.BlockSpec((tf, D_MODEL), lambda i, j: (once(i, j), 0)),
        ]
    return pl.pallas_call(
        functools.partial(_ffn_kernel, n_prev=len(prev), n_cast=len(casts), emit_h=emit_h,
                          f32_weights=f32_weights),
        grid=(n_tiles, nj),
        in_specs=[pl.BlockSpec(memory_space=pl.ANY)] * len(prev)
        + [tile(x_tile0), vec, vec, vec] + w_specs + cast_specs,
        out_specs=out_specs,
        out_shape=out_shape,
        scratch_shapes=[pltpu.VMEM((tm, D_MODEL), BF16), pltpu.VMEM((tm, D_MODEL), F32)],
        input_output_aliases={k: k for k in range(len(prev))},
        compiler_params=_cparams(("arbitrary", "arbitrary"), FFN_VMEM_LIMIT),
        name=name,
    )(*prev, x, pre, post, nxt, wg, wu, wd, *[w for w, _, _ in casts])


WIDE_TN = 1024


def _dot_nt(a, b_t):
    return lax.dot_general(a, b_t, (((1,), (1,)), ((), ())), preferred_element_type=F32)


def _mm_plain_kernel(h_ref, w_ref, o_ref):
    o_ref[...] = _dot_nt(h_ref[...], w_ref[...]).astype(o_ref.dtype)


def _z_dt_kernel(h_ref, wz_ref, wdt_ref, b_ref, z_ref, dt_ref):
    h = h_ref[...]
    z_ref[...] = _dot_nt(h, wz_ref[...])

    @pl.when(pl.program_id(1) == 0)
    def _():
        dt = jax.nn.softplus(_dot_nt(h, wdt_ref[...]) + b_ref[...])
        lane = lax.broadcasted_iota(jnp.int32, dt.shape, 1)
        dt_ref[...] = jnp.where(lane < HEADS, dt, 0.0)


def _proj_z_dt(h, wt, dt_row0, dt_bias, *, tm=1024, tn=WIDE_TN):
    m, k = h.shape
    assert dt_row0 % LANES == 0
    return pl.pallas_call(
        _z_dt_kernel,
        grid=(m // tm, DINNER // tn),
        in_specs=[
            pl.BlockSpec((tm, k), lambda i, j: (i, 0)),
            pl.BlockSpec((tn, k), lambda i, j: (j, 0)),
            pl.BlockSpec((LANES, k), lambda i, j: (dt_row0 // LANES, 0)),
            pl.BlockSpec((1, LANES), lambda i, j: (0, 0)),
        ],
        out_specs=[
            pl.BlockSpec((tm, tn), lambda i, j: (i, j)),
            pl.BlockSpec((tm, LANES), lambda i, j: (i, 0)),
        ],
        out_shape=[jax.ShapeDtypeStruct((m, DINNER), F32), jax.ShapeDtypeStruct((m, LANES), F32)],
        compiler_params=_cparams(("parallel", "arbitrary")),
        name="proj_z_dt",
    )(h, wt, wt, dt_bias)


def _mm_glu_kernel(h_ref, wa_ref, wb_ref, o_ref):
    h = h_ref[...]
    o_ref[...] = _dot_nt(h, wa_ref[...]) * jax.nn.sigmoid(_dot_nt(h, wb_ref[...]))


def _mm(kernel, h, ws, n_out, out_dtype, name, *, tm=1024, tn=512):
    m, k = h.shape
    tn = min(tn, n_out)
    grid = (m // tm, n_out // tn)
    in_specs = [pl.BlockSpec((tm, k), lambda i, j: (i, 0))]
    for _, r0 in ws:
        assert r0 % tn == 0
        in_specs.append(pl.BlockSpec((tn, k), functools.partial(lambda i, j, o: (o + j, 0), o=r0 // tn)))
    return pl.pallas_call(
        kernel,
        grid=grid,
        in_specs=in_specs,
        out_specs=pl.BlockSpec((tm, tn), lambda i, j: (i, j)),
        out_shape=jax.ShapeDtypeStruct((m, n_out), out_dtype),
        compiler_params=_cparams(("parallel", "arbitrary")),
        name=name,
    )(h, *[w for w, _ in ws])


def _block_masks(seg_len):
    t = lax.broadcasted_iota(jnp.int32, (CHUNK, CHUNK), 0)
    s = lax.broadcasted_iota(jnp.int32, (CHUNK, CHUNK), 1)
    if seg_len == CHUNK:
        same = t >= 0
    else:
        sh = seg_len.bit_length() - 1
        same = (t >> sh) == (s >> sh)
    causal = jnp.logical_and(same, s <= t)
    return same, causal


CONV_STRIP = 512


def _conv4_silu(full, w8_ref, cb_ref, cols):
    n = full.shape[-2] - SUBLANES
    width = full.shape[-1]
    acc = None
    for k in range(SSM_K):
        r0 = SUBLANES - (SSM_K - 1) + k
        win = full[..., r0:r0 + n, :].reshape(CHUNK // SUBLANES, SUBLANES, width)
        t = w8_ref[k, :, cols][None] * win
        acc = t if acc is None else acc + t
    return jax.nn.silu(acc.reshape(CHUNK, width) + cb_ref[:, cols])


def _conv4_strips(load_strip, w8_ref, cb_ref):
    strips = []
    for c0 in range(0, CONV_DIM, CONV_STRIP):
        cols = slice(c0, c0 + CONV_STRIP)
        strips.append(_conv4_silu(load_strip(cols), w8_ref, cb_ref, cols))
    return strips


def _ssd_intra(strips, dt, alog_row, e01, seg_len):
    same, causal = _block_masks(seg_len)
    n_x = DINNER // CONV_STRIP
    xs = jnp.concatenate(strips[:n_x], axis=1)
    bm = strips[n_x]
    cm = strips[n_x + 1]
    a_row = -jnp.exp(alog_row)
    dta = dt * a_row
    causal01 = jnp.where(causal, 1.0, 0.0).astype(BF16)
    same01 = jnp.where(same, 1.0, 0.0).astype(BF16)
    cum = _dot01_lhs(causal01, dta)
    cum_end = _dot01_lhs(same01, dta)
    cum_t = cum.T
    dt_t = dt.T
    ecum_x = _dot01_rhs(jnp.exp(cum), e01)
    coef_x = _dot01_rhs(jnp.exp(cum_end - cum) * dt, e01)

    lane = lax.broadcasted_iota(jnp.int32, (CHUNK, LANES), 1)
    first_head = lane < HEADDIM
    ys = []
    for g in range(GROUPS):
        cg = cm[:, g * STATE:(g + 1) * STATE].astype(BF16)
        bg = bm[:, g * STATE:(g + 1) * STATE].astype(BF16)
        cb = lax.dot_general(cg, bg, (((1,), (1,)), ((), ())), preferred_element_type=F32)
        for p in range(HPG // 2):
            ws = []
            for e in (g * HPG + 2 * p, g * HPG + 2 * p + 1):
                seg = cum[:, e:e + 1] - cum_t[e:e + 1, :]
                dec = jnp.exp(jnp.where(causal, seg, -jnp.inf))
                ws.append(((cb * dec) * dt_t[e:e + 1, :]).astype(BF16))
            w2 = jnp.concatenate(ws, axis=1)
            col = (g * HPG + 2 * p) * HEADDIM
            x2 = xs[:, col:col + LANES]
            r2 = jnp.concatenate([jnp.where(first_head, x2, 0.0),
                                  jnp.where(first_head, 0.0, x2)], axis=0).astype(BF16)
            ys.append(_dot(w2, r2))
    y_diag = jnp.concatenate(ys, axis=1)
    return xs, cm, bm, y_diag, coef_x, ecum_x


def _store_gated_norm(yn_ref, y, z_ref, nrm_ref):
    for r in range(0, CHUNK, BF16_ROWS):
        rows = slice(r, r + BF16_ROWS)
        yn_ref[rows, :] = _rms(y[rows, :] * jax.nn.silu(z_ref[rows, :]), nrm_ref[...]).astype(BF16)


def _ssd_prompt_kernel(xbc_ref, dt_ref, z_ref, w8_ref, cb_ref, e_ref, alog_ref, dx_ref, nrm_ref,
                       yn_ref, hout_ref, xp_ref, ht_ref):
    c = pl.program_id(1)

    @pl.when(c == 0)
    def _():
        xp_ref[0:SUBLANES, :] = jnp.zeros((SUBLANES, CONV_DIM), F32)
        ht_ref[...] = jnp.zeros_like(ht_ref)

    xp_ref[SUBLANES:SUBLANES + CHUNK, :] = xbc_ref[...]
    strips = _conv4_strips(lambda cols: xp_ref[:, cols], w8_ref, cb_ref)
    xp_ref[0:SUBLANES, :] = xbc_ref[CHUNK - SUBLANES:CHUNK, :]

    xs, cm, bm, y_diag, coef_x, ecum_x = _ssd_intra(strips, dt_ref[...], alog_ref[...], e_ref[...], CHUNK)
    xcoef = (xs * coef_x).astype(BF16)
    dec_row = ecum_x[CHUNK - 1:CHUNK, :]
    y_off = []
    for g in range(GROUPS):
        gs = slice(g * GW, (g + 1) * GW)
        ht_g = ht_ref[:, gs]
        cg = cm[:, g * STATE:(g + 1) * STATE].astype(BF16)
        y_off.append(_dot(cg, ht_g.astype(BF16)))
        bg_t = bm[:, g * STATE:(g + 1) * STATE].T.astype(BF16)
        ht_ref[:, gs] = ht_g * dec_row[:, gs] + _dot(bg_t, xcoef[:, gs])
    y = y_diag + jnp.concatenate(y_off, axis=1) * ecum_x + dx_ref[...] * xs
    _store_gated_norm(yn_ref, y, z_ref, nrm_ref)

    @pl.when(c == pl.num_programs(1) - 1)
    def _():
        hout_ref[...] = ht_ref[...].T


def _ssm_param_specs(const):
    return [
        pl.BlockSpec((SSM_K, SUBLANES, CONV_DIM), lambda *_: (0, 0, 0)),
        pl.BlockSpec((1, CONV_DIM), const),
        pl.BlockSpec((3 * LANES, DINNER), const),
        pl.BlockSpec((1, LANES), const),
        pl.BlockSpec((1, DINNER), const),
        pl.BlockSpec((1, DINNER), const),
    ]


def _ssd_prompt(xbc, dt, z, *params):
    nc = SEQ // CHUNK
    blk = lambda b, c: (b * nc + c, 0)
    const = lambda b, c: (0, 0)
    return pl.pallas_call(
        _ssd_prompt_kernel,
        grid=(BATCH, nc),
        in_specs=[
            pl.BlockSpec((CHUNK, CONV_DIM), blk),
            pl.BlockSpec((CHUNK, LANES), blk),
            pl.BlockSpec((CHUNK, DINNER), blk),
        ] + _ssm_param_specs(const),
        out_specs=[
            pl.BlockSpec((CHUNK, DINNER), blk),
            pl.BlockSpec((None, DINNER, STATE), lambda b, c: (b, 0, 0)),
        ],
        out_shape=[
            jax.ShapeDtypeStruct((N_TOK, DINNER), BF16),
            jax.ShapeDtypeStruct((BATCH, DINNER, STATE), F32),
        ],
        scratch_shapes=[pltpu.VMEM((CHUNK + SUBLANES, CONV_DIM), F32), pltpu.VMEM((STATE, DINNER), F32)],
        compiler_params=_cparams(("parallel", "arbitrary")),
        name="ssd_prompt",
    )(xbc, dt, z, *params)


def _ssd_sample_kernel(yn_any, xbc_ref, cst_ref, dt_ref, z_ref, h0_ref, w8_ref, cb_ref, e_ref,
                       alog_ref, dx_ref, nrm_ref, yn_ref, hout_ref,
                       xp_ref, cm_ref, bt_ref, xcoef_ref, ecx_ref, y_ref):
    del yn_any
    j = pl.program_id(1)

    @pl.when(j == 0)
    def _():
        xp_ref[:, SUBLANES - (SSM_K - 1):SUBLANES, :] = cst_ref[...]
        xp_ref[:, SUBLANES:SUBLANES + DEC_SEQ, :] = xbc_ref[...]
        strips = _conv4_strips(lambda cols: xp_ref[:, :, cols], w8_ref, cb_ref)
        xs, cm, bm, y_diag, coef_x, ecum_x = _ssd_intra(
            strips, dt_ref[...], alog_ref[...], e_ref[...], DEC_SEQ)
        cm_ref[...] = cm
        for g in range(GROUPS):
            bt_ref[g] = bm[:, g * STATE:(g + 1) * STATE].T
        xcoef_ref[...] = (xs * coef_x).astype(BF16)
        ecx_ref[...] = ecum_x
        y_ref[...] = y_diag + dx_ref[...] * xs

    col = lax.broadcasted_iota(jnp.int32, (STATE, CHUNK), 1)
    base = j * (SEQ_PER_STEP * DEC_SEQ)
    for p in range(SEQ_PER_STEP):
        r0 = pl.multiple_of(base + p * DEC_SEQ, DEC_SEQ)
        rows = pl.ds(r0, DEC_SEQ)
        ht = h0_ref[p].T
        ht_b = ht.astype(BF16)
        ecx = ecx_ref[rows, :]
        cj = cm_ref[rows, :].astype(BF16)
        mine = jnp.logical_and(col >= r0, col < r0 + DEC_SEQ)
        y_off, new = [], []
        for g in range(GROUPS):
            gs = slice(g * GW, (g + 1) * GW)
            y_off.append(_dot(cj[:, g * STATE:(g + 1) * STATE], ht_b[:, gs]))
            bg_t = jnp.where(mine, bt_ref[g], 0.0).astype(BF16)
            new.append(_dot(bg_t, xcoef_ref[:, gs]))
        y_ref[rows, :] += jnp.concatenate(y_off, axis=1) * ecx
        ht_new = ht * ecx[DEC_SEQ - 1:DEC_SEQ, :] + jnp.concatenate(new, axis=1)
        hout_ref[p] = ht_new.T

    @pl.when(j == pl.num_programs(1) - 1)
    def _():
        _store_gated_norm(yn_ref, y_ref, z_ref, nrm_ref)


def _ssd_sample(yn_all, xbc3, cst, dt, z, h0, *params):
    nb = DEC_BATCH // SEGS
    off = N_PROMPT // CHUNK
    const = lambda i, j: (0, 0)
    tok = lambda i, j: (off + i, 0)
    seq = lambda i, j: (i * (SEGS // SEQ_PER_STEP) + j, 0, 0)
    return pl.pallas_call(
        _ssd_sample_kernel,
        grid=(nb, SEGS // SEQ_PER_STEP),
        in_specs=[
            pl.BlockSpec(memory_space=pl.ANY),
            pl.BlockSpec((SEGS, DEC_SEQ, CONV_DIM), lambda i, j: (off + i, 0, 0)),
            pl.BlockSpec((SEGS, SSM_K - 1, CONV_DIM), lambda i, j: (i, 0, 0)),
            pl.BlockSpec((CHUNK, LANES), tok),
            pl.BlockSpec((CHUNK, DINNER), tok),
            pl.BlockSpec((SEQ_PER_STEP, DINNER, STATE), seq),
        ] + _ssm_param_specs(const),
        out_specs=[
            pl.BlockSpec((CHUNK, DINNER), tok),
            pl.BlockSpec((SEQ_PER_STEP, DINNER, STATE), seq),
        ],
        out_shape=[
            jax.ShapeDtypeStruct((N_TOK, DINNER), BF16),
            jax.ShapeDtypeStruct((DEC_BATCH, DINNER, STATE), F32),
        ],
        scratch_shapes=[
            pltpu.VMEM((SEGS, SUBLANES + DEC_SEQ, CONV_DIM), F32),
            pltpu.VMEM((CHUNK, GROUPS * STATE), F32),
            pltpu.VMEM((GROUPS, STATE, CHUNK), F32),
            pltpu.VMEM((CHUNK, DINNER), BF16),
            pltpu.VMEM((CHUNK, DINNER), F32),
            pltpu.VMEM((CHUNK, DINNER), F32),
        ],
        input_output_aliases={0: 0},
        compiler_params=_cparams(("parallel", "arbitrary")),
        name="ssd_sample",
    )(yn_all, xbc3, cst, dt, z, h0, *params)


CC_PAD = 32
CC_FIRST = CC_PAD - (CC_K - 1)
CC_LN_ROWS = 32
CC_STRIP = 512


def _cc_kernel(*refs, n_seg, seg_len, carry, aliased):
    refs = list(refs)
    if aliased:
        refs.pop(0)
    u_ref = refs.pop(0)
    st_ref = None if carry else refs.pop(0)
    w8_ref, b_ref, g_ref, be_ref, o_ref, xp_ref, res_ref = refs
    if carry:
        @pl.when(pl.program_id(1) == 0)
        def _():
            xp_ref[:, 0:CC_PAD, :] = jnp.zeros((n_seg, CC_PAD, CC_DIM), F32)
    else:
        xp_ref[:, CC_FIRST:CC_PAD, :] = st_ref[...]
    xp_ref[:, CC_PAD:CC_PAD + seg_len, :] = u_ref[...]

    rb = min(seg_len, 64)
    n_shift = rb + CC_PAD - SUBLANES
    for s in range(n_seg):
        for t0 in range(0, seg_len, rb):
            for c0 in range(0, CC_DIM, CC_STRIP):
                cols = slice(c0, c0 + CC_STRIP)
                full = xp_ref[s, t0:t0 + rb + CC_PAD, cols]
                acc = None
                for b in range(SUBLANES):
                    xb = full if b == 0 else full[b:b + n_shift, :]
                    for a in range(CC_PAD // SUBLANES + 1):
                        k = SUBLANES * a + b - CC_FIRST
                        if k < 0 or k >= CC_K:
                            continue
                        win = xb[SUBLANES * a:SUBLANES * a + rb, :].reshape(rb // SUBLANES, SUBLANES, CC_STRIP)
                        t = w8_ref[k, :, cols][None] * win
                        acc = t if acc is None else acc + t
                row = s * seg_len + t0
                res_ref[row:row + rb, cols] = acc.reshape(rb, CC_STRIP) + b_ref[:, cols]

    for r0 in range(0, n_seg * seg_len, CC_LN_ROWS):
        v = res_ref[r0:r0 + CC_LN_ROWS, :]
        mu = jnp.mean(v, axis=-1, keepdims=True)
        d = v - mu
        var = jnp.mean(d * d, axis=-1, keepdims=True)
        yv = (d * lax.rsqrt(var + EPS)) * g_ref[...] + be_ref[...]
        o_ref[r0:r0 + CC_LN_ROWS, :] = jax.nn.silu(yv).astype(BF16)
    if carry:
        xp_ref[:, 0:CC_PAD, :] = xp_ref[:, seg_len:seg_len + CC_PAD, :]


def _cc_param_specs(const):
    return [
        pl.BlockSpec((CC_K, SUBLANES, CC_DIM), lambda *_: (0, 0, 0)),
        pl.BlockSpec((1, CC_DIM), const),
        pl.BlockSpec((1, CC_DIM), const),
        pl.BlockSpec((1, CC_DIM), const),
    ]


def _cc_prompt(u3, *params, tq):
    nc = SEQ // tq
    const = lambda i, c: (0, 0)
    return pl.pallas_call(
        functools.partial(_cc_kernel, n_seg=1, seg_len=tq, carry=True, aliased=False),
        grid=(BATCH, nc),
        in_specs=[pl.BlockSpec((1, tq, CC_DIM), lambda i, c: (i * nc + c, 0, 0))] + _cc_param_specs(const),
        out_specs=pl.BlockSpec((tq, CC_DIM), lambda i, c: (i * nc + c, 0)),
        out_shape=jax.ShapeDtypeStruct((N_TOK, CC_DIM), BF16),
        scratch_shapes=[pltpu.VMEM((1, CC_PAD + tq, CC_DIM), F32), pltpu.VMEM((tq, CC_DIM), F32)],
        compiler_params=_cparams(("parallel", "arbitrary")),
        name="cc_prompt",
    )(u3, *params)


def _cc_sample(uc_all, u3, st, *params, n_seg):
    const = lambda i: (0, 0)
    rows = n_seg * DEC_SEQ
    off = N_PROMPT // rows
    return pl.pallas_call(
        functools.partial(_cc_kernel, n_seg=n_seg, seg_len=DEC_SEQ, carry=False, aliased=True),
        grid=(DEC_BATCH // n_seg,),
        in_specs=[
            pl.BlockSpec(memory_space=pl.ANY),
            pl.BlockSpec((n_seg, DEC_SEQ, CC_DIM), lambda i: (off + i, 0, 0)),
            pl.BlockSpec((n_seg, CC_K - 1, CC_DIM), lambda i: (i, 0, 0)),
        ] + _cc_param_specs(const),
        out_specs=pl.BlockSpec((rows, CC_DIM), lambda i: (off + i, 0)),
        out_shape=jax.ShapeDtypeStruct((N_TOK, CC_DIM), BF16),
        scratch_shapes=[pltpu.VMEM((n_seg, CC_PAD + DEC_SEQ, CC_DIM), F32),
                        pltpu.VMEM((rows, CC_DIM), F32)],
        input_output_aliases={0: 0},
        compiler_params=_cparams(("parallel",)),
        name="cc_sample",
    )(uc_all, u3, st, *params)


def _merge_kernel(h_ref, yn_ref, uc_ref, wgs_ref, wgc_ref, wso_ref, wco_ref, o_ref):
    h = h_ref[...]
    g_ssd = jax.nn.sigmoid(_dot_nt(h, wgs_ref[...]))
    g_cc = jax.nn.sigmoid(_dot_nt(h, wgc_ref[...]))
    ssd_out = _dot(yn_ref[...], wso_ref[...])
    cc_out = _dot(uc_ref[...], wco_ref[...])
    o_ref[...] = (g_ssd * ssd_out + g_cc * cc_out).astype(BF16)


def _merge(h, yn, uc, wgate_t, wso, wco, *, tm=1024, tn=512):
    m = h.shape[0]
    row = lambda i, j: (i, 0)
    colw = lambda i, j: (0, j)
    roww = lambda i, j: (j, 0)
    roww2 = lambda i, j: (D_MODEL // tn + j, 0)
    return pl.pallas_call(
        _merge_kernel,
        grid=(m // tm, D_MODEL // tn),
        in_specs=[
            pl.BlockSpec((tm, D_MODEL), row),
            pl.BlockSpec((tm, DINNER), row),
            pl.BlockSpec((tm, CC_DIM), row),
            pl.BlockSpec((tn, D_MODEL), roww),
            pl.BlockSpec((tn, D_MODEL), roww2),
            pl.BlockSpec((DINNER, tn), colw),
            pl.BlockSpec((CC_DIM, tn), colw),
        ],
        out_specs=pl.BlockSpec((tm, tn), lambda i, j: (i, j)),
        out_shape=jax.ShapeDtypeStruct((m, D_MODEL), BF16),
        compiler_params=_cparams(("parallel", "arbitrary")),
        name="merge",
    )(h, yn, uc, wgate_t, wgate_t, wso, wco)


def _oproj_kernel(m_ref, x_ref, w_ref, nrm_ref, o_ref):
    o_ref[...] = x_ref[...] + _rms(_dot(m_ref[...], w_ref[...]), nrm_ref[...])


def _oproj(mg, x, w, nrm, *, tm=512):
    m = x.shape[0]
    row = lambda i: (i, 0)
    const = lambda i: (0, 0)
    return pl.pallas_call(
        _oproj_kernel,
        grid=(m // tm,),
        in_specs=[
            pl.BlockSpec((tm, D_MODEL), row),
            pl.BlockSpec((tm, D_MODEL), row),
            pl.BlockSpec((D_MODEL, D_MODEL), const),
            pl.BlockSpec((1, D_MODEL), const),
        ],
        out_specs=pl.BlockSpec((tm, D_MODEL), row),
        out_shape=jax.ShapeDtypeStruct((m, D_MODEL), F32),
        compiler_params=_cparams(("parallel",)),
        name="oproj",
    )(mg, x, w, nrm)


def _row(v):
    return v.reshape(1, -1).astype(F32)


def _pad_lanes(v):
    return jnp.pad(v.astype(F32), (0, LANES - v.shape[0])).reshape(1, LANES)


def _taps_on_sublanes(w):
    return jnp.broadcast_to(w.astype(F32)[:, None, :], (w.shape[0], SUBLANES, w.shape[1]))


def kernel(x_prompt, x_sample, state_ssm, state_ssm_conv, state_cc_conv, ffn1_pre_norm, ffn1_post_norm, ffn1_w_gate, ffn1_w_up, ffn1_w_down, mix_pre_norm, mix_post_norm, w_in, ssm_conv_w, ssm_conv_b, ssm_dt_bias, ssm_A_log, ssm_D, ssm_norm, w_ssd_out, cc_conv_w, cc_conv_b, cc_ln_g, cc_ln_b, w_cc_out, w_o, ffn2_pre_norm, ffn2_post_norm, ffn2_w_gate, ffn2_w_up, ffn2_w_down):
    l = 0
    npt = N_PROMPT // FFN_TM

    f1 = (_row(ffn1_pre_norm[l]), _row(ffn1_post_norm[l]), _row(mix_pre_norm[l]))
    x1, h, w1g, w1u, w1d = _ffn(
        x_sample.reshape(N_SAMPLE, D_MODEL), 0, 1, *f1, ffn1_w_gate[l], ffn1_w_up[l], ffn1_w_down[l],
        out_rows=N_TOK, out_tile0=N_PROMPT // N_SAMPLE, emit_h=True, name="ffn1_sample",
        tm=N_SAMPLE, tf=FFN_TF // 2)
    w_in_t = w_in[l].T
    c_xbc = DINNER
    c_dt = c_xbc + CONV_DIM
    c_glu = c_dt + HEADS
    c_gate = c_glu + 2 * CC_DIM
    whole = lambda w: (w, 0, w.shape[0])
    later = (whole(ffn2_w_gate[l]), whole(ffn2_w_up[l]), whole(ffn2_w_down[l]),
             (w_in_t, 0, c_dt + LANES), (w_in_t, c_glu, 2 * CC_DIM), (w_in_t, c_gate, 2 * D_MODEL),
             whole(w_ssd_out[l]), whole(w_cc_out[l]), whole(w_o[l]))
    x1, h, w2g, w2u, w2d, wt, wt_glu, wt_gate, w_so, w_co, w_ob = _ffn(
        x_prompt.reshape(N_PROMPT, D_MODEL), 0, npt, *f1, w1g, w1u, w1d, out_rows=N_TOK, out_tile0=0,
        emit_h=True, prev=(x1, h), casts=later, name="ffn1_prompt")

    e01 = (jnp.arange(LANES)[:, None] == (jnp.arange(DINNER)[None, :] // HEADDIM)).astype(BF16)
    e01 = jnp.concatenate([e01] * 3, axis=0)
    d_x = jnp.repeat(ssm_D[l].astype(F32), HEADDIM).reshape(1, DINNER)

    z, dt = _proj_z_dt(h, wt, c_dt, _pad_lanes(ssm_dt_bias[l]))
    xbc = _mm(_mm_plain_kernel, h, [(wt, c_xbc)], CONV_DIM, F32, "proj_xbc", tn=WIDE_TN)
    u = _mm(_mm_glu_kernel, h, [(wt_glu, 0), (wt_glu, CC_DIM)], CC_DIM, F32, "proj_glu", tn=WIDE_TN)

    ssm_params = (_taps_on_sublanes(ssm_conv_w[l]), _row(ssm_conv_b[l]), e01, _pad_lanes(ssm_A_log[l]),
                  d_x, _row(ssm_norm[l]))
    yn, ssm_p = _ssd_prompt(xbc, dt, z, *ssm_params)
    yn, ssm_s = _ssd_sample(yn, xbc.reshape(N_TOK // DEC_SEQ, DEC_SEQ, CONV_DIM), state_ssm_conv[l], dt, z,
                            state_ssm[l].reshape(DEC_BATCH, DINNER, STATE), *ssm_params)

    cc_params = (_taps_on_sublanes(cc_conv_w[l]), _row(cc_conv_b[l]), _row(cc_ln_g[l]), _row(cc_ln_b[l]))
    tq = 256
    uc = _cc_prompt(u.reshape(N_TOK // tq, tq, CC_DIM), *cc_params, tq=tq)
    uc = _cc_sample(uc, u.reshape(N_TOK // DEC_SEQ, DEC_SEQ, CC_DIM), state_cc_conv[l], *cc_params, n_seg=16)

    mg = _merge(h, yn, uc, wt_gate, w_so, w_co)
    x2 = _oproj(mg, x1, w_ob, _row(mix_post_norm[l]))

    f2 = (_row(ffn2_pre_norm[l]), _row(ffn2_post_norm[l]), _row(ffn2_post_norm[l]), w2g, w2u, w2d)
    (y_p,) = _ffn(x2, 0, npt, *f2, out_rows=N_PROMPT, out_tile0=0, emit_h=False, name="ffn2_prompt")
    (y_s,) = _ffn(x2, N_PROMPT // N_SAMPLE, 1, *f2, out_rows=N_SAMPLE, out_tile0=0, emit_h=False,
                  name="ffn2_sample", tm=N_SAMPLE)

    y_prompt = y_p.reshape(BATCH, SEQ, D_MODEL)
    y_sample = y_s.reshape(DEC_BATCH, DEC_SEQ, D_MODEL)
    new_ssm_p = ssm_p.reshape(1, BATCH, HEADS, HEADDIM, STATE)
    new_ssm_s = ssm_s.reshape(1, DEC_BATCH, HEADS, HEADDIM, STATE)
    tail = lambda a, n: jnp.stack([a[(b + 1) * SEQ - n:(b + 1) * SEQ] for b in range(BATCH)])[None]
    new_sconv_p = tail(xbc, SSM_K - 1)
    new_cc_p = tail(u, CC_K - 1)
    xbc_s3 = xbc[N_PROMPT:].reshape(DEC_BATCH, DEC_SEQ, CONV_DIM)
    new_sconv_s = xbc_s3[:, DEC_SEQ - (SSM_K - 1):][None]
    u_s3 = u[N_PROMPT:].reshape(DEC_BATCH, DEC_SEQ, CC_DIM)
    new_cc_s = jnp.concatenate([state_cc_conv[l][:, DEC_SEQ:], u_s3], axis=1)[None]
    return (y_prompt, y_sample, new_ssm_p, new_sconv_p, new_cc_p, new_ssm_s, new_sconv_s, new_cc_s)
```
